```python
import jax, jax.numpy as jnp
from jax import lax
import numpy as np

D_MODEL = 1024
BATCH = 32
SEQ = 2048
DEPTH = 1

PLE_DIM = 256
MIX_WIDTH = D_MODEL
ATTN_WIDTH = MIX_WIDTH // 2
CONV_WIDTH = MIX_WIDTH - ATTN_WIDTH
V_DIM = 64
NOPE_DIM = 64
ROPE_DIM = 32
N_ATTN_HEADS = ATTN_WIDTH // V_DIM
Q_RANK = D_MODEL // 4
KV_RANK = D_MODEL // 4
N_CONV_GROUPS = 8
CONV_K = 3
IN_COLS = Q_RANK + KV_RANK + ROPE_DIM + 3 * CONV_WIDTH
D_FF = ((8 * D_MODEL // 3 + 255) // 256) * 256
ROPE_THETA = 10000.0
Q_BLOCK = 128
EPS = 1e-6

kernel_name = "hybrid_mla_shortconv_parallel_layer"


def rmsnorm(x, g):
    xf = x.astype(jnp.float32)
    y = xf * lax.rsqrt(jnp.mean(xf * xf, axis=-1, keepdims=True) + EPS)
    return (y * g.astype(jnp.float32)).astype(x.dtype)


def rope_tables(positions, dtype):
    half = ROPE_DIM // 2
    inv_freq = 1.0 / (ROPE_THETA ** (jnp.arange(half, dtype=jnp.float32) / half))
    ang = positions.astype(jnp.float32)[..., None] * inv_freq
    return jnp.cos(ang).astype(dtype), jnp.sin(ang).astype(dtype)


def apply_rope(t, cos, sin):
    t1, t2 = jnp.split(t, 2, axis=-1)
    return jnp.concatenate([t1 * cos - t2 * sin, t2 * cos + t1 * sin], axis=-1)


def causal_block_attention(q, k, v):
    B, S, H, Dq = q.shape
    Dv = v.shape[-1]
    nb = S // Q_BLOCK
    scale = Dq ** -0.5
    qb = q.reshape(B, nb, Q_BLOCK, H, Dq).transpose(1, 0, 2, 3, 4)
    k_idx = jnp.arange(S)

    def one_block(args):
        q_blk, blk = args
        q_idx = blk * Q_BLOCK + jnp.arange(Q_BLOCK)
        s = jnp.einsum('bqhd,bkhd->bhqk', q_blk, k).astype(jnp.float32) * scale
        s = jnp.where(k_idx[None, :] <= q_idx[:, None], s, jnp.finfo(jnp.float32).min)
        pr = jax.nn.softmax(s, axis=-1)
        return jnp.einsum('bhqk,bkhd->bqhd', pr.astype(v.dtype), v)

    out = lax.map(one_block, (qb, jnp.arange(nb)))
    return out.transpose(1, 0, 2, 3, 4).reshape(B, S, H * Dv)


def mla(c_q_raw, c_kv_raw, k_rope_raw, positions, q_norm, kv_norm, w_uq, w_ukv):
    B, S, _ = c_q_raw.shape
    H = N_ATTN_HEADS
    q = (rmsnorm(c_q_raw, q_norm) @ w_uq).reshape(B, S, H, NOPE_DIM + ROPE_DIM)
    kv = (rmsnorm(c_kv_raw, kv_norm) @ w_ukv).reshape(B, S, H, NOPE_DIM + V_DIM)
    k_nope, v = kv[..., :NOPE_DIM], kv[..., NOPE_DIM:]
    cos, sin = rope_tables(positions, q.dtype)
    q_rope = apply_rope(q[..., NOPE_DIM:], cos[:, :, None, :], sin[:, :, None, :])
    k_rope = apply_rope(k_rope_raw, cos, sin)
    q_full = jnp.concatenate([q[..., :NOPE_DIM], q_rope], axis=-1)
    k_full = jnp.concatenate(
        [k_nope, jnp.broadcast_to(k_rope[:, :, None, :], (B, S, H, ROPE_DIM))], axis=-1)
    return causal_block_attention(q_full, k_full, v)


def short_gated_conv(b_gate, c_gate, xv, conv_w):
    S = xv.shape[1]
    u = c_gate * xv
    u_pad = jnp.pad(u, ((0, 0), (CONV_K - 1, 0), (0, 0)))
    y = conv_w[0] * u_pad[:, CONV_K - 1:CONV_K - 1 + S]
    for k in range(1, CONV_K):
        y = y + conv_w[k] * u_pad[:, CONV_K - 1 - k:CONV_K - 1 - k + S]
    return b_gate * y


def setup_inputs(seed: int = 0) -> dict:
    key = jax.random.key(seed)
    ks = jax.random.split(key, 24)
    f32 = jnp.float32

    def w(k, shape, fan_in):
        return jax.random.normal(k, shape, f32) * (fan_in ** -0.5)

    def gain(k, shape):
        return 1.0 + 0.05 * jax.random.normal(k, shape, f32)

    L = DEPTH
    x = jax.random.normal(ks[0], (BATCH, SEQ, D_MODEL), f32)
    p = jax.random.normal(ks[1], (DEPTH, BATCH, SEQ, PLE_DIM), f32)
    offset = jax.random.randint(ks[2], (BATCH, 1), 0, 4096)
    positions = (offset + jnp.arange(SEQ, dtype=jnp.int32)[None, :]).astype(jnp.int32)
    return {
        "x": x,
        "p": p,
        "positions": positions,
        "mix_pre_norm": gain(ks[3], (L, D_MODEL)),
        "w_in": w(ks[4], (L, D_MODEL, IN_COLS), D_MODEL),
        "q_norm": gain(ks[5], (L, Q_RANK)),
        "kv_norm": gain(ks[6], (L, KV_RANK)),
        "w_uq": w(ks[7], (L, Q_RANK, N_ATTN_HEADS * (NOPE_DIM + ROPE_DIM)), Q_RANK),
        "w_ukv": w(ks[8], (L, KV_RANK, N_ATTN_HEADS * (NOPE_DIM + V_DIM)), KV_RANK),
        "conv_w": w(ks[9], (L, CONV_K, CONV_WIDTH), CONV_K),
        "attn_group_norm": gain(ks[10], (L, ATTN_WIDTH)),
        "conv_group_norm": gain(ks[11], (L, CONV_WIDTH)),
        "w_out": w(ks[12], (L, MIX_WIDTH, D_MODEL), MIX_WIDTH),
        "mix_post_norm": gain(ks[13], (L, D_MODEL)),
        "ffn_pre_norm": gain(ks[14], (L, D_MODEL)),
        "w_gate": w(ks[15], (L, D_MODEL, D_FF), D_MODEL),
        "w_up": w(ks[16], (L, D_MODEL, D_FF), D_MODEL),
        "w_down": w(ks[17], (L, D_FF, D_MODEL), D_FF),
        "ffn_post_norm": gain(ks[18], (L, D_MODEL)),
        "w_ple_proj": w(ks[19], (L, PLE_DIM, D_MODEL), PLE_DIM),
        "ple_norm": gain(ks[20], (L, D_MODEL)),
        "w_ple_gate": w(ks[21], (L, D_MODEL, D_MODEL), D_MODEL),
    }


def reference(x, p, positions, mix_pre_norm, w_in, q_norm, kv_norm, w_uq, w_ukv,
              conv_w, attn_group_norm, conv_group_norm, w_out, mix_post_norm,
              ffn_pre_norm, w_gate, w_up, w_down, ffn_post_norm,
              w_ple_proj, ple_norm, w_ple_gate):
    o1 = Q_RANK
    o2 = o1 + KV_RANK
    o3 = o2 + ROPE_DIM
    o4 = o3 + CONV_WIDTH
    o5 = o4 + CONV_WIDTH
    for i in range(DEPTH):
        h = rmsnorm(x, mix_pre_norm[i])
        z = h @ w_in[i]
        a = mla(z[..., :o1], z[..., o1:o2], z[..., o2:o3], positions,
                q_norm[i], kv_norm[i], w_uq[i], w_ukv[i])
        c = short_gated_conv(z[..., o3:o4], z[..., o4:o5], z[..., o5:], conv_w[i])
        m = jnp.concatenate([rmsnorm(a, attn_group_norm[i]),
                             rmsnorm(c, conv_group_norm[i])], axis=-1)
        x = x + rmsnorm(m @ w_out[i], mix_post_norm[i])
        h = rmsnorm(x, ffn_pre_norm[i])
        f = (jax.nn.silu(h @ w_gate[i]) * (h @ w_up[i])) @ w_down[i]
        x = x + rmsnorm(f, ffn_post_norm[i])
        e = rmsnorm(p[i] @ w_ple_proj[i], ple_norm[i])
        x = x + jax.nn.sigmoid(x @ w_ple_gate[i]) * e
    return x
```

```python
import functools

import numpy as np
import jax
import jax.numpy as jnp
from jax import lax
from jax.experimental import pallas as pl
from jax.experimental.pallas import tpu as pltpu

D_MODEL = 1024
PLE_DIM = 256
ATTN_WIDTH = 512
CONV_WIDTH = 512
V_DIM = 64
NOPE_DIM = 64
ROPE_DIM = 32
HALF_ROPE = ROPE_DIM // 2
N_HEADS = 8
Q_RANK = 256
KV_RANK = 256
CONV_K = 3
D_FF = 2816
ROPE_THETA = 10000.0
EPS = 1e-6

LANES = 128
QK_WIDTH = N_HEADS * LANES
HEADS_PER_STEP = 2
FF_CHUNK = 256

PROJ_TM = 512
ATTN_TQ = 256
POST_TM = 512
VMEM_LIMIT = 56 * 1024 * 1024

_O_CQ = 0
_O_CKV = _O_CQ + Q_RANK
_O_KRA = _O_CKV + KV_RANK
_O_KRB = _O_KRA + LANES
_O_BG = _O_KRB + LANES
_O_CG = _O_BG + CONV_WIDTH
_O_XV = _O_CG + CONV_WIDTH
IN_EXT = _O_XV + CONV_WIDTH


def _rms(x, g):
    ms = jnp.mean(x * x, axis=-1, keepdims=True)
    return x * lax.rsqrt(ms + EPS) * g


def _dot(a, b):
    return jnp.dot(a, b, preferred_element_type=jnp.float32)


def _dot_nt(a, b):
    return lax.dot_general(a, b, (((1,), (1,)), ((), ())),
                           preferred_element_type=jnp.float32)


def _proj_kernel(x_ref, pos_ref, gpre_ref, win_ref, qn_ref, kvn_ref, wq_ref,
                 wk_ref, wv_ref, cw_ref, cn_ref, invf_ref, sgn_ref,
                 q_ref, k_ref, v_ref, c_ref, utail_ref):
    tm = x_ref.shape[1]
    bf16 = jnp.bfloat16

    @pl.when(pl.program_id(1) == 0)
    def _():
        utail_ref[...] = jnp.zeros_like(utail_ref)

    h = _rms(x_ref[0], gpre_ref[...]).astype(bf16)
    z = _dot(h, win_ref[...])

    ang = pos_ref[0] * invf_ref[...]
    cos = jnp.cos(ang)
    sin = jnp.sin(ang) * sgn_ref[...]
    scale = (NOPE_DIM + ROPE_DIM) ** -0.5
    cos_q = cos * scale
    sin_q = sin * scale

    cq = _rms(z[:, _O_CQ:_O_CQ + Q_RANK], qn_ref[...]).astype(bf16)
    qq = _dot(cq, wq_ref[...])
    ckv = _rms(z[:, _O_CKV:_O_CKV + KV_RANK], kvn_ref[...]).astype(bf16)
    kk = _dot(ckv, wk_ref[...])
    v_ref[0] = _dot(ckv, wv_ref[...]).astype(bf16)
    kr = z[:, _O_KRA:_O_KRA + LANES] * cos + z[:, _O_KRB:_O_KRB + LANES] * sin
    for hh in range(N_HEADS):
        lo = hh * LANES
        qa = qq[:, lo:lo + LANES]
        qb = qq[:, QK_WIDTH + lo:QK_WIDTH + lo + LANES]
        q_ref[0, :, lo:lo + LANES] = (qa * cos_q + qb * sin_q).astype(bf16)
        k_ref[0, :, lo:lo + LANES] = (kk[:, lo:lo + LANES] + kr).astype(bf16)

    u = z[:, _O_CG:_O_CG + CONV_WIDTH] * z[:, _O_XV:_O_XV + CONV_WIDTH]
    tail = utail_ref[...]
    row = lax.broadcasted_iota(jnp.int32, (tm, CONV_WIDTH), 0)
    u1 = pltpu.roll(u, 1, axis=0)
    u1 = jnp.where(row == 0, tail[7:8, :], u1)
    u2 = pltpu.roll(u, 2, axis=0)
    u2 = jnp.where(row == 0, tail[6:7, :], jnp.where(row == 1, tail[7:8, :], u2))
    utail_ref[...] = u[tm - 8:tm, :]
    cw = cw_ref[...]
    y = cw[0:1, :] * u + cw[1:2, :] * u1 + cw[2:3, :] * u2
    cc = z[:, _O_BG:_O_BG + CONV_WIDTH] * y
    c_ref[0] = _rms(cc, cn_ref[...]).astype(bf16)


def _const_spec(shape):
    nd = len(shape)
    return pl.BlockSpec(shape, lambda *_: (0,) * nd, pipeline_mode=pl.Buffered(1))


def _proj_call(x, posf, gpre, win, qn, kvn, wq, wk, wv, cw, cn, invf, sgn):
    B, S, _ = x.shape
    tm = PROJ_TM
    grid = (B, S // tm)
    tok = lambda w: pl.BlockSpec((1, tm, w), lambda b, s: (b, s, 0))
    consts = (gpre, win, qn, kvn, wq, wk, wv, cw, cn, invf, sgn)
    bf16 = jnp.bfloat16
    return pl.pallas_call(
        _proj_kernel,
        grid=grid,
        in_specs=[tok(D_MODEL), tok(1)] + [_const_spec(c.shape) for c in consts],
        out_specs=[tok(QK_WIDTH), tok(QK_WIDTH), tok(ATTN_WIDTH), tok(CONV_WIDTH)],
        out_shape=[jax.ShapeDtypeStruct((B, S, QK_WIDTH), bf16),
                   jax.ShapeDtypeStruct((B, S, QK_WIDTH), bf16),
                   jax.ShapeDtypeStruct((B, S, ATTN_WIDTH), bf16),
                   jax.ShapeDtypeStruct((B, S, CONV_WIDTH), bf16)],
        scratch_shapes=[pltpu.VMEM((8, CONV_WIDTH), jnp.float32)],
        compiler_params=pltpu.CompilerParams(
            dimension_semantics=("arbitrary", "arbitrary"),
            vmem_limit_bytes=VMEM_LIMIT),
        name="proj",
    )(x, posf, *consts)


def _attn_kernel(q_ref, k_ref, v_ref, o_ref):
    S = q_ref.shape[1]
    tq = ATTN_TQ
    bf16 = jnp.bfloat16
    neg = jnp.finfo(jnp.float32).min
    row = lax.broadcasted_iota(jnp.int32, (tq, tq), 0)
    col = lax.broadcasted_iota(jnp.int32, (tq, tq), 1)
    causal = col <= row
    first_head = lax.broadcasted_iota(jnp.int32, (tq, LANES), 1) < V_DIM

    for i in range(S // tq):
        l0 = i * tq
        outs = []
        for hh in range(HEADS_PER_STEP):
            hs = slice(hh * LANES, (hh + 1) * LANES)
            q = q_ref[0, l0:l0 + tq, hs]
            sd = _dot_nt(q, k_ref[0, l0:l0 + tq, hs])
            sd = jnp.where(causal, sd, neg)
            m = jnp.max(sd, axis=-1, keepdims=True)
            if i > 0:
                sm = _dot_nt(q, k_ref[0, 0:l0, hs])
                m = jnp.maximum(m, jnp.max(sm, axis=-1, keepdims=True))
                pm = jnp.exp(sm - m)
                l = jnp.sum(pm, axis=-1, keepdims=True)
                o = _dot(pm.astype(bf16), v_ref[0, 0:l0, :])
            pd = jnp.exp(sd - m)
            if i > 0:
                l = l + jnp.sum(pd, axis=-1, keepdims=True)
                o = o + _dot(pd.astype(bf16), v_ref[0, l0:l0 + tq, :])
            else:
                l = jnp.sum(pd, axis=-1, keepdims=True)
                o = _dot(pd.astype(bf16), v_ref[0, l0:l0 + tq, :])
            outs.append(o * (1.0 / l))
        o_ref[0, l0:l0 + tq, :] = jnp.where(first_head, outs[0], outs[1])


def _attn_call(q, k, v):
    B, S, _ = q.shape
    grid = (B, N_HEADS // HEADS_PER_STEP)
    qk_spec = pl.BlockSpec((1, S, HEADS_PER_STEP * LANES), lambda b, h: (b, 0, h))
    v_spec = pl.BlockSpec((1, S, LANES), lambda b, h: (b, 0, h))
    return pl.pallas_call(
        _attn_kernel,
        grid=grid,
        in_specs=[qk_spec, qk_spec, v_spec],
        out_specs=v_spec,
        out_shape=jax.ShapeDtypeStruct((B, S, ATTN_WIDTH), jnp.float32),
        compiler_params=pltpu.CompilerParams(
            dimension_semantics=("arbitrary", "arbitrary"),
            vmem_limit_bytes=VMEM_LIMIT),
        name="attn",
    )(q, k, v)


def _post_kernel(x_ref, a_ref, c_ref, p_ref, an_ref, wo_ref, mpn_ref, fpre_ref,
                 wg_ref, wu_ref, wd_ref, fpost_ref, wpp_ref, pn_ref, wpg_ref,
                 o_ref):
    bf16 = jnp.bfloat16
    a = _rms(a_ref[...], an_ref[...]).astype(bf16)
    m = jnp.concatenate([a, c_ref[...]], axis=-1)
    x1 = x_ref[...] + _rms(_dot(m, wo_ref[...]), mpn_ref[...])

    h = _rms(x1, fpre_ref[...]).astype(bf16)
    f = None
    for j in range(D_FF // FF_CHUNK):
        cs = slice(j * FF_CHUNK, (j + 1) * FF_CHUNK)
        g = _dot(h, wg_ref[:, cs])
        u = _dot(h, wu_ref[:, cs])
        act = (g * jax.nn.sigmoid(g) * u).astype(bf16)
        part = _dot(act, wd_ref[cs, :])
        f = part if f is None else f + part
    x2 = x1 + _rms(f, fpost_ref[...])

    e = _rms(_dot(p_ref[...].astype(bf16), wpp_ref[...]), pn_ref[...])
    gate = jax.nn.sigmoid(_dot(x2.astype(bf16), wpg_ref[...]))
    o_ref[...] = x2 + gate * e


def _post_call(x2d, a2d, c2d, p2d, an, wo, mpn, fpre, wg, wu, wd, fpost, wpp, pn, wpg):
    T = x2d.shape[0]
    tm = POST_TM
    tok = lambda w: pl.BlockSpec((tm, w), lambda t: (t, 0))
    consts = (an, wo, mpn, fpre, wg, wu, wd, fpost, wpp, pn, wpg)
    return pl.pallas_call(
        _post_kernel,
        grid=(T // tm,),
        in_specs=[tok(D_MODEL), tok(ATTN_WIDTH), tok(CONV_WIDTH), tok(PLE_DIM)]
        + [_const_spec(c.shape) for c in consts],
        out_specs=tok(D_MODEL),
        out_shape=jax.ShapeDtypeStruct((T, D_MODEL), jnp.float32),
        compiler_params=pltpu.CompilerParams(
            dimension_semantics=("arbitrary",),
            vmem_limit_bytes=VMEM_LIMIT),
        name="post",
    )(x2d, a2d, c2d, p2d, *consts)


def _rope_lane_tables():
    inv_freq = 1.0 / (ROPE_THETA ** (np.arange(HALF_ROPE, dtype=np.float32) / HALF_ROPE))
    invf = np.zeros((1, LANES), np.float32)
    sgn = np.zeros((1, LANES), np.float32)
    invf[0, NOPE_DIM:NOPE_DIM + HALF_ROPE] = inv_freq
    invf[0, NOPE_DIM + HALF_ROPE:NOPE_DIM + ROPE_DIM] = inv_freq
    sgn[0, NOPE_DIM:NOPE_DIM + HALF_ROPE] = -1.0
    sgn[0, NOPE_DIM + HALF_ROPE:NOPE_DIM + ROPE_DIM] = 1.0
    return jnp.asarray(invf), jnp.asarray(sgn)


def _swap_halves(w):
    return jnp.concatenate([w[..., HALF_ROPE:], w[..., :HALF_ROPE]], axis=-1)


def _prep_w_in(w_in):
    o2 = Q_RANK + KV_RANK
    o3 = o2 + ROPE_DIM
    kr = w_in[:, o2:o3]
    zl = jnp.zeros((D_MODEL, NOPE_DIM), w_in.dtype)
    zr = jnp.zeros((D_MODEL, LANES - NOPE_DIM - ROPE_DIM), w_in.dtype)
    kra = jnp.concatenate([zl, kr, zr], axis=1)
    krb = jnp.concatenate([zl, _swap_halves(kr), zr], axis=1)
    return jnp.concatenate([w_in[:, :o2], kra, krb, w_in[:, o3:]], axis=1).astype(jnp.bfloat16)


def _prep_w_uq(w_uq):
    w = w_uq.reshape(Q_RANK, N_HEADS, NOPE_DIM + ROPE_DIM)
    nope, rope = w[..., :NOPE_DIM], w[..., NOPE_DIM:]
    pad = jnp.zeros((Q_RANK, N_HEADS, LANES - NOPE_DIM - ROPE_DIM), w.dtype)
    wa = jnp.concatenate([nope, rope, pad], axis=-1).reshape(Q_RANK, QK_WIDTH)
    wb = jnp.concatenate([jnp.zeros_like(nope), _swap_halves(rope), pad], axis=-1)
    wb = wb.reshape(Q_RANK, QK_WIDTH)
    return jnp.concatenate([wa, wb], axis=1).astype(jnp.bfloat16)


def _prep_w_ukv(w_ukv):
    w = w_ukv.reshape(KV_RANK, N_HEADS, NOPE_DIM + V_DIM)
    k_nope, v = w[..., :NOPE_DIM], w[..., NOPE_DIM:]
    pad = jnp.zeros((KV_RANK, N_HEADS, LANES - NOPE_DIM), w.dtype)
    wk = jnp.concatenate([k_nope, pad], axis=-1).reshape(KV_RANK, QK_WIDTH)
    wv = v.reshape(KV_RANK, ATTN_WIDTH)
    return wk.astype(jnp.bfloat16), wv.astype(jnp.bfloat16)


def kernel(x, p, positions, mix_pre_norm, w_in, q_norm, kv_norm, w_uq, w_ukv, conv_w, attn_group_norm, conv_group_norm, w_out, mix_post_norm, ffn_pre_norm, w_gate, w_up, w_down, ffn_post_norm, w_ple_proj, ple_norm, w_ple_gate):
    B, S, D = x.shape
    depth = w_in.shape[0]
    bf16 = jnp.bfloat16
    row = lambda g: g.reshape(1, -1)
    invf, sgn = _rope_lane_tables()
    posf = positions.astype(jnp.float32)[..., None]
    for i in range(depth):
        wk, wv = _prep_w_ukv(w_ukv[i])
        q, k, v, c = _proj_call(
            x, posf, row(mix_pre_norm[i]), _prep_w_in(w_in[i]), row(q_norm[i]),
            row(kv_norm[i]), _prep_w_uq(w_uq[i]), wk, wv, conv_w[i],
            row(conv_group_norm[i]), invf, sgn)
        a = _attn_call(q, k, v)
        out = _post_call(
            x.reshape(B * S, D), a.reshape(B * S, ATTN_WIDTH),
            c.reshape(B * S, CONV_WIDTH), p[i].reshape(B * S, PLE_DIM),
            row(attn_group_norm[i]), w_out[i].astype(bf16), row(mix_post_norm[i]),
            row(ffn_pre_norm[i]), w_gate[i].astype(bf16), w_up[i].astype(bf16),
            w_down[i].astype(bf16), row(ffn_post_norm[i]),
            w_ple_proj[i].astype(bf16), row(ple_norm[i]), w_ple_gate[i].astype(bf16))
        x = out.reshape(B, S, D)
    return x
```

```python
import functools

import numpy as np
import jax
import jax.numpy as jnp
from jax import lax
from jax.experimental import pallas as pl
from jax.experimental.pallas import tpu as pltpu

D_MODEL = 1024
PLE_DIM = 256
ATTN_WIDTH = 512
CONV_WIDTH = 512
V_DIM = 64
NOPE_DIM = 64
ROPE_DIM = 32
HALF_ROPE = ROPE_DIM // 2
N_HEADS = 8
Q_RANK = 256
KV_RANK = 256
CONV_K = 3
D_FF = 2816
ROPE_THETA = 10000.0
EPS = 1e-6
LOG2_E = 1.4426950408889634

LANES = 128
QK_WIDTH = N_HEADS * LANES
HEADS_PER_STEP = 2
SUM_ROWS = 16
FF_CHUNK = 256

PROJ_TM = 512
ATTN_TQ = 256
POST_TM = 512
VMEM_LIMIT = 56 * 1024 * 1024

_O_CQ = 0
_O_CKV = _O_CQ + Q_RANK
_O_KRA = _O_CKV + KV_RANK
_O_KRB = _O_KRA + LANES
_O_BG = _O_KRB + LANES
_O_CG = _O_BG + CONV_WIDTH
_O_XV = _O_CG + CONV_WIDTH
IN_EXT = _O_XV + CONV_WIDTH


def _rms(x, g):
    ms = jnp.mean(x * x, axis=-1, keepdims=True)
    return x * lax.rsqrt(ms + EPS) * g


def _dot(a, b):
    return jnp.dot(a, b, preferred_element_type=jnp.float32)


def _dot_nt(a, b):
    return lax.dot_general(a, b, (((1,), (1,)), ((), ())),
                           preferred_element_type=jnp.float32)


def _proj_kernel(x_ref, pos_ref, gpre_ref, win_ref, qn_ref, kvn_ref, wq_ref,
                 wk_ref, wv_ref, cw_ref, cn_ref, invf_ref,
                 q_ref, k_ref, v_ref, c_ref, utail_ref):
    tm = x_ref.shape[1]
    bf16 = jnp.bfloat16

    @pl.when(pl.program_id(1) == 0)
    def _():
        utail_ref[...] = jnp.zeros_like(utail_ref)

    h = _rms(x_ref[0], gpre_ref[...]).astype(bf16)
    z = _dot(h, win_ref[...])

    ang_t = invf_ref[...] * pos_ref[0]
    cos_t = jnp.cos(ang_t)
    sin_t = jnp.sin(ang_t)
    pad = jnp.zeros((LANES - NOPE_DIM - ROPE_DIM, tm), jnp.float32)
    cos = jnp.concatenate(
        [jnp.ones((NOPE_DIM, tm), jnp.float32), cos_t, cos_t, pad], axis=0).T
    sin = jnp.concatenate(
        [jnp.zeros((NOPE_DIM, tm), jnp.float32), -sin_t, sin_t, pad], axis=0).T
    scale = (NOPE_DIM + ROPE_DIM) ** -0.5 * LOG2_E
    cos_q = cos * scale
    sin_q = sin * scale

    cq = _rms(z[:, _O_CQ:_O_CQ + Q_RANK], qn_ref[...]).astype(bf16)
    qq = _dot(cq, wq_ref[...])
    ckv = _rms(z[:, _O_CKV:_O_CKV + KV_RANK], kvn_ref[...]).astype(bf16)
    kk = _dot(ckv, wk_ref[...])
    v_ref[0] = _dot_nt(wv_ref[...], ckv).astype(bf16)
    kr = z[:, _O_KRA:_O_KRA + LANES] * cos + z[:, _O_KRB:_O_KRB + LANES] * sin
    for hh in range(N_HEADS):
        lo = hh * LANES
        qa = qq[:, lo:lo + LANES]
        qb = qq[:, QK_WIDTH + lo:QK_WIDTH + lo + LANES]
        q_ref[0, :, lo:lo + LANES] = (qa * cos_q + qb * sin_q).astype(bf16)
        k_ref[0, :, lo:lo + LANES] = (kk[:, lo:lo + LANES] + kr).astype(bf16)

    u = z[:, _O_CG:_O_CG + CONV_WIDTH] * z[:, _O_XV:_O_XV + CONV_WIDTH]
    tail = utail_ref[...]
    row = lax.broadcasted_iota(jnp.int32, (tm, CONV_WIDTH), 0)
    u1 = pltpu.roll(u, 1, axis=0)
    u1 = jnp.where(row == 0, tail[7:8, :], u1)
    u2 = pltpu.roll(u, 2, axis=0)
    u2 = jnp.where(row == 0, tail[6:7, :], jnp.where(row == 1, tail[7:8, :], u2))
    utail_ref[...] = u[tm - 8:tm, :]
    cw = cw_ref[...]
    y = cw[0:1, :] * u + cw[1:2, :] * u1 + cw[2:3, :] * u2
    cc = z[:, _O_BG:_O_BG + CONV_WIDTH] * y
    c_ref[0] = _rms(cc, cn_ref[...]).astype(bf16)


def _const_spec(shape):
    nd = len(shape)
    return pl.BlockSpec(shape, lambda *_: (0,) * nd, pipeline_mode=pl.Buffered(1))


def _proj_call(x, posf, gpre, win, qn, kvn, wq, wk, wv, cw, cn, invf):
    B, S, _ = x.shape
    tm = PROJ_TM
    grid = (B, S // tm)
    tok = lambda w: pl.BlockSpec((1, tm, w), lambda b, s: (b, s, 0))
    pos_spec = pl.BlockSpec((1, 1, tm), lambda b, s: (b, 0, s))
    consts = (gpre, win, qn, kvn, wq, wk, wv, cw, cn, invf)
    bf16 = jnp.bfloat16
    return pl.pallas_call(
        _proj_kernel,
        grid=grid,
        in_specs=[tok(D_MODEL), pos_spec] + [_const_spec(c.shape) for c in consts],
        out_specs=[tok(QK_WIDTH), tok(QK_WIDTH),
                   pl.BlockSpec((1, ATTN_WIDTH, tm), lambda b, s: (b, 0, s)),
                   tok(CONV_WIDTH)],
        out_shape=[jax.ShapeDtypeStruct((B, S, QK_WIDTH), bf16),
                   jax.ShapeDtypeStruct((B, S, QK_WIDTH), bf16),
                   jax.ShapeDtypeStruct((B, ATTN_WIDTH, S), bf16),
                   jax.ShapeDtypeStruct((B, S, CONV_WIDTH), bf16)],
        scratch_shapes=[pltpu.VMEM((8, CONV_WIDTH), jnp.float32)],
        compiler_params=pltpu.CompilerParams(
            dimension_semantics=("arbitrary", "arbitrary"),
            vmem_limit_bytes=VMEM_LIMIT),
        name="proj",
    )(x, posf, *consts)


_DONE = object()


def _attn_kernel(q_ref, k_ref, vt_ref, o_ref, s_scr, p_scr):
    S = q_ref.shape[1]
    tq = ATTN_TQ
    bf16 = jnp.bfloat16
    neg = jnp.finfo(jnp.float32).min
    key = lax.broadcasted_iota(jnp.int32, (tq, tq), 0)
    qry = lax.broadcasted_iota(jnp.int32, (tq, tq), 1)
    causal = key <= qry
    ones = jnp.ones((SUM_ROWS, tq), bf16)
    bodies = [(i, hh) for i in range(S // tq) for hh in range(HEADS_PER_STEP)]
    col_max = {}
    head_out = {}

    def chunk(c):
        return slice(c * tq, (c + 1) * tq)

    def scores(n):
        i, hh = bodies[n]
        hs = slice(hh * LANES, (hh + 1) * LANES)
        q = q_ref[0, chunk(i), hs]
        part = None
        for c in range(i + 1):
            s = _dot_nt(k_ref[0, chunk(c), hs], q)
            if c == i:
                s = jnp.where(causal, s, neg)
            s_scr[n % 2, chunk(c), :] = s
            cm = jnp.max(s.reshape(tq // 8, 8, tq), axis=0)
            part = cm if part is None else jnp.maximum(part, cm)
            if c == i:
                col_max[n] = jnp.max(part, axis=0, keepdims=True)
            yield

    def exps(n):
        i, _ = bodies[n]
        m = col_max.pop(n)
        for c in range(i + 1):
            p_scr[n % 2, chunk(c), :] = jnp.exp2(s_scr[n % 2, chunk(c), :] - m).astype(bf16)
            yield

    def values(n):
        i, hh = bodies[n]
        acc = None
        for c in range(i + 1):
            vt = jnp.concatenate(
                [vt_ref[0, hh * V_DIM:(hh + 1) * V_DIM, chunk(c)], ones], axis=0)
            d = _dot(vt, p_scr[n % 2, chunk(c), :])
            acc = d if acc is None else acc + d
            if c == i:
                head_out[hh] = acc[0:V_DIM, :] * (1.0 / acc[V_DIM:V_DIM + 1, :])
                if hh == HEADS_PER_STEP - 1:
                    o_ref[0, chunk(i), :] = jnp.concatenate(
                        [head_out.pop(h) for h in range(HEADS_PER_STEP)], axis=0).T
            yield

    for step in range(len(bodies) + 2):
        live = []
        if step < len(bodies):
            live.append(scores(step))
        if 0 <= step - 1 < len(bodies):
            live.append(exps(step - 1))
        if 0 <= step - 2 < len(bodies):
            live.append(values(step - 2))
        while live:
            live = [g for g in live if next(g, _DONE) is not _DONE]


def _attn_call(q, k, vt):
    B, S, _ = q.shape
    grid = (B, N_HEADS // HEADS_PER_STEP)
    qk_spec = pl.BlockSpec((1, S, HEADS_PER_STEP * LANES), lambda b, h: (b, 0, h))
    vt_spec = pl.BlockSpec((1, HEADS_PER_STEP * V_DIM, S), lambda b, h: (b, h, 0))
    o_spec = pl.BlockSpec((1, S, HEADS_PER_STEP * V_DIM), lambda b, h: (b, 0, h))
    return pl.pallas_call(
        _attn_kernel,
        grid=grid,
        in_specs=[qk_spec, qk_spec, vt_spec],
        out_specs=o_spec,
        out_shape=jax.ShapeDtypeStruct((B, S, ATTN_WIDTH), jnp.float32),
        scratch_shapes=[pltpu.VMEM((2, S, ATTN_TQ), jnp.float32),
                        pltpu.VMEM((2, S, ATTN_TQ), jnp.bfloat16)],
        compiler_params=pltpu.CompilerParams(
            dimension_semantics=("arbitrary", "arbitrary"),
            vmem_limit_bytes=VMEM_LIMIT),
        name="attn",
    )(q, k, vt)


def _post_kernel(x_ref, a_ref, c_ref, p_ref, an_ref, wo_ref, mpn_ref, fpre_ref,
                 wg_ref, wu_ref, wd_ref, fpost_ref, wpp_ref, pn_ref, wpg_ref,
                 o_ref):
    bf16 = jnp.bfloat16
    a = _rms(a_ref[...], an_ref[...]).astype(bf16)
    m = jnp.concatenate([a, c_ref[...]], axis=-1)
    x1 = x_ref[...] + _rms(_dot(m, wo_ref[...]), mpn_ref[...])

    h = _rms(x1, fpre_ref[...]).astype(bf16)
    f = None
    for j in range(D_FF // FF_CHUNK):
        cs = slice(j * FF_CHUNK, (j + 1) * FF_CHUNK)
        g = _dot(h, wg_ref[:, cs])
        u = _dot(h, wu_ref[:, cs])
        act = (g * jax.nn.sigmoid(g) * u).astype(bf16)
        part = _dot(act, wd_ref[cs, :])
        f = part if f is None else f + part
    x2 = x1 + _rms(f, fpost_ref[...])

    e = _rms(_dot(p_ref[...].astype(bf16), wpp_ref[...]), pn_ref[...])
    gate = jax.nn.sigmoid(_dot(x2.astype(bf16), wpg_ref[...]))
    o_ref[...] = x2 + gate * e


def _post_call(x2d, a2d, c2d, p2d, an, wo, mpn, fpre, wg, wu, wd, fpost, wpp, pn, wpg):
    T = x2d.shape[0]
    tm = POST_TM
    tok = lambda w: pl.BlockSpec((tm, w), lambda t: (t, 0))
    consts = (an, wo, mpn, fpre, wg, wu, wd, fpost, wpp, pn, wpg)
    return pl.pallas_call(
        _post_kernel,
        grid=(T // tm,),
        in_specs=[tok(D_MODEL), tok(ATTN_WIDTH), tok(CONV_WIDTH), tok(PLE_DIM)]
        + [_const_spec(c.shape) for c in consts],
        out_specs=tok(D_MODEL),
        out_shape=jax.ShapeDtypeStruct((T, D_MODEL), jnp.float32),
        compiler_params=pltpu.CompilerParams(
            dimension_semantics=("arbitrary",),
            vmem_limit_bytes=VMEM_LIMIT),
        name="post",
    )(x2d, a2d, c2d, p2d, *consts)


def _inv_freq_column():
    inv_freq = 1.0 / (ROPE_THETA ** (np.arange(HALF_ROPE, dtype=np.float32) / HALF_ROPE))
    return jnp.asarray(inv_freq.astype(np.float32).reshape(HALF_ROPE, 1))


def _swap_halves(w):
    return jnp.concatenate([w[..., HALF_ROPE:], w[..., :HALF_ROPE]], axis=-1)


def _prep_w_in(w_in):
    o2 = Q_RANK + KV_RANK
    o3 = o2 + ROPE_DIM
    kr = w_in[:, o2:o3]
    zl = jnp.zeros((D_MODEL, NOPE_DIM), w_in.dtype)
    zr = jnp.zeros((D_MODEL, LANES - NOPE_DIM - ROPE_DIM), w_in.dtype)
    kra = jnp.concatenate([zl, kr, zr], axis=1)
    krb = jnp.concatenate([zl, _swap_halves(kr), zr], axis=1)
    return jnp.concatenate([w_in[:, :o2], kra, krb, w_in[:, o3:]], axis=1).astype(jnp.bfloat16)


def _prep_w_uq(w_uq):
    w = w_uq.reshape(Q_RANK, N_HEADS, NOPE_DIM + ROPE_DIM)
    nope, rope = w[..., :NOPE_DIM], w[..., NOPE_DIM:]
    pad = jnp.zeros((Q_RANK, N_HEADS, LANES - NOPE_DIM - ROPE_DIM), w.dtype)
    wa = jnp.concatenate([nope, rope, pad], axis=-1).reshape(Q_RANK, QK_WIDTH)
    wb = jnp.concatenate([jnp.zeros_like(nope), _swap_halves(rope), pad], axis=-1)
    wb = wb.reshape(Q_RANK, QK_WIDTH)
    return jnp.concatenate([wa, wb], axis=1).astype(jnp.bfloat16)


def _prep_w_ukv(w_ukv):
    w = w_ukv.reshape(KV_RANK, N_HEADS, NOPE_DIM + V_DIM)
    k_nope, v = w[..., :NOPE_DIM], w[..., NOPE_DIM:]
    pad = jnp.zeros((KV_RANK, N_HEADS, LANES - NOPE_DIM), w.dtype)
    wk = jnp.concatenate([k_nope, pad], axis=-1).reshape(KV_RANK, QK_WIDTH)
    wv_t = v.reshape(KV_RANK, ATTN_WIDTH).T
    return wk.astype(jnp.bfloat16), wv_t.astype(jnp.bfloat16)


def kernel(x, p, positions, mix_pre_norm, w_in, q_norm, kv_norm, w_uq, w_ukv, conv_w, attn_group_norm, conv_group_norm, w_out, mix_post_norm, ffn_pre_norm, w_gate, w_up, w_down, ffn_post_norm, w_ple_proj, ple_norm, w_ple_gate):
    B, S, D = x.shape
    depth = w_in.shape[0]
    bf16 = jnp.bfloat16
    row = lambda g: g.reshape(1, -1)
    invf = _inv_freq_column()
    posf = positions.astype(jnp.float32)[:, None, :]
    for i in range(depth):
        wk, wv = _prep_w_ukv(w_ukv[i])
        q, k, v, c = _proj_call(
            x, posf, row(mix_pre_norm[i]), _prep_w_in(w_in[i]), row(q_norm[i]),
            row(kv_norm[i]), _prep_w_uq(w_uq[i]), wk, wv, conv_w[i],
            row(conv_group_norm[i]), invf)
        a = _attn_call(q, k, v)
        out = _post_call(
            x.reshape(B * S, D), a.reshape(B * S, ATTN_WIDTH),
            c.reshape(B * S, CONV_WIDTH), p[i].reshape(B * S, PLE_DIM),
            row(attn_group_norm[i]), w_out[i].astype(bf16), row(mix_post_norm[i]),
            row(ffn_pre_norm[i]), w_gate[i].astype(bf16), w_up[i].astype(bf16),
            w_down[i].astype(bf16), row(ffn_post_norm[i]),
            w_ple_proj[i].astype(bf16), row(ple_norm[i]), w_ple_gate[i].astype(bf16))
        x = out.reshape(B, S, D)
    return x
```

```python
import functools

import numpy as np
import jax
import jax.numpy as jnp
from jax import lax
from jax.experimental import pallas as pl
from jax.experimental.pallas import tpu as pltpu

D_MODEL = 1024
PLE_DIM = 256
ATTN_WIDTH = 512
CONV_WIDTH = 512
V_DIM = 64
NOPE_DIM = 64
ROPE_DIM = 32
HALF_ROPE = ROPE_DIM // 2
N_HEADS = 8
Q_RANK = 256
KV_RANK = 256
CONV_K = 3
D_FF = 2816
ROPE_THETA = 10000.0
EPS = 1e-6
LOG2_E = 1.4426950408889634

LANES = 128
HALF_LANES = LANES // 2
QK_WIDTH = N_HEADS * LANES
HEADS_PER_STEP = 2
N_PAIRS = N_HEADS // HEADS_PER_STEP
PAIR_LANES = HEADS_PER_STEP * LANES
SUM_ROWS = 16
V_ROWS = V_DIM + SUM_ROWS
VT_WIDTH = N_HEADS * V_ROWS
FF_CHUNK = 256

PROJ_TM = 1024
PROJ_ROW_GROUPS = 2
ATTN_TQ = 256
POST_TM = 512
VMEM_LIMIT = 56 * 1024 * 1024

_O_CQ = 0
_O_CKV = _O_CQ + Q_RANK
_O_KR = _O_CKV + KV_RANK
_O_BG = _O_KR + LANES
_O_CG = _O_BG + CONV_WIDTH
_O_XV = _O_CG + CONV_WIDTH
IN_EXT = _O_XV + CONV_WIDTH


_DONE = object()


def _rms(x, g):
    ms = jnp.mean(x * x, axis=-1, keepdims=True)
    return x * lax.rsqrt(ms + EPS) * g


def _dot(a, b):
    return jnp.dot(a, b, preferred_element_type=jnp.float32)


def _dot_nt(a, b):
    return lax.dot_general(a, b, (((1,), (1,)), ((), ())),
                           preferred_element_type=jnp.float32)


def _proj_kernel(x_ref, pos_ref, gpre_ref, win_ref, qn_ref, kvn_ref, wq_ref,
                 wk_ref, wv_ref, vone_ref, cw_ref, cn_ref, invf_ref,
                 q_ref, k_ref, v_ref, c_ref, utail_ref):
    tm = x_ref.shape[1]
    bf16 = jnp.bfloat16
    rg = tm // PROJ_ROW_GROUPS

    @pl.when(pl.program_id(1) == 0)
    def _():
        utail_ref[...] = jnp.zeros_like(utail_ref)

    def rows_pipeline(rows):
        h = _rms(x_ref[0, rows, :], gpre_ref[...]).astype(bf16)
        yield
        z = _dot(h, win_ref[...])
        yield
        cq = _rms(z[:, _O_CQ:_O_CQ + Q_RANK], qn_ref[...]).astype(bf16)
        qq = _dot(cq, wq_ref[...])
        ckv = _rms(z[:, _O_CKV:_O_CKV + KV_RANK], kvn_ref[...]).astype(bf16)
        kk = _dot(ckv, wk_ref[...])
        v_ref[0, :, rows] = (_dot_nt(wv_ref[...], ckv) + vone_ref[...]).astype(bf16)

        ang_t = invf_ref[...] * pos_ref[0, :, rows]
        cos_t = jnp.cos(ang_t)
        sin_t = jnp.sin(ang_t)
        one_t = jnp.ones((HALF_LANES - HALF_ROPE, rg), jnp.float32)
        zero_t = jnp.zeros((HALF_LANES - HALF_ROPE, rg), jnp.float32)
        cos = jnp.concatenate([cos_t, one_t, cos_t, one_t], axis=0).T
        sin = jnp.concatenate([-sin_t, zero_t, sin_t, zero_t], axis=0).T
        scale = (NOPE_DIM + ROPE_DIM) ** -0.5 * LOG2_E
        cos_q = cos * scale
        sin_q = sin * scale
        kr = z[:, _O_KR:_O_KR + LANES]
        kr = kr * cos + pltpu.roll(kr, HALF_LANES, axis=1) * sin

        u = z[:, _O_CG:_O_CG + CONV_WIDTH] * z[:, _O_XV:_O_XV + CONV_WIDTH]
        tail = utail_ref[...]
        row = lax.broadcasted_iota(jnp.int32, (rg, CONV_WIDTH), 0)
        u1 = pltpu.roll(u, 1, axis=0)
        u1 = jnp.where(row == 0, tail[7:8, :], u1)
        u2 = pltpu.roll(u, 2, axis=0)
        u2 = jnp.where(row == 0, tail[6:7, :], jnp.where(row == 1, tail[7:8, :], u2))
        utail_ref[...] = u[rg - 8:rg, :]
        cw = cw_ref[...]
        y = cw[0:1, :] * u + cw[1:2, :] * u1 + cw[2:3, :] * u2
        cc = z[:, _O_BG:_O_BG + CONV_WIDTH] * y
        c_ref[0, rows, :] = _rms(cc, cn_ref[...]).astype(bf16)
        yield

        for hh in range(N_HEADS):
            lo = hh * LANES
            qh = qq[:, lo:lo + LANES]
            qh = qh * cos_q + pltpu.roll(qh, HALF_LANES, axis=1) * sin_q
            pair, ps = divmod(hh, HEADS_PER_STEP)
            pl_ = slice(ps * LANES, (ps + 1) * LANES)
            q_ref[0, pair, rows, pl_] = qh.astype(bf16)
            k_ref[0, pair, rows, pl_] = (kk[:, lo:lo + LANES] + kr).astype(bf16)

    live = []
    for r in range(PROJ_ROW_GROUPS):
        live.insert(0, rows_pipeline(slice(r * rg, (r + 1) * rg)))
        for _ in range(2):
            live = [g for g in live if next(g, _DONE) is not _DONE]
    while live:
        live = [g for g in live if next(g, _DONE) is not _DONE]


def _const_spec(shape):
    nd = len(shape)
    return pl.BlockSpec(shape, lambda *_: (0,) * nd, pipeline_mode=pl.Buffered(1))


def _proj_call(x, posf, gpre, win, qn, kvn, wq, wk, wv, vone, cw, cn, invf):
    B, S, _ = x.shape
    tm = PROJ_TM
    grid = (B, S // tm)
    tok = lambda w: pl.BlockSpec((1, tm, w), lambda b, s: (b, s, 0))
    pos_spec = pl.BlockSpec((1, 1, tm), lambda b, s: (b, 0, s))
    qk_spec = pl.BlockSpec((1, N_PAIRS, tm, PAIR_LANES), lambda b, s: (b, 0, s, 0))
    consts = (gpre, win, qn, kvn, wq, wk, wv, vone, cw, cn, invf)
    bf16 = jnp.bfloat16
    return pl.pallas_call(
        _proj_kernel,
        grid=grid,
        in_specs=[tok(D_MODEL), pos_spec] + [_const_spec(c.shape) for c in consts],
        out_specs=[qk_spec, qk_spec,
                   pl.BlockSpec((1, VT_WIDTH, tm), lambda b, s: (b, 0, s)),
                   tok(CONV_WIDTH)],
        out_shape=[jax.ShapeDtypeStruct((B, N_PAIRS, S, PAIR_LANES), bf16),
                   jax.ShapeDtypeStruct((B, N_PAIRS, S, PAIR_LANES), bf16),
                   jax.ShapeDtypeStruct((B, VT_WIDTH, S), bf16),
                   jax.ShapeDtypeStruct((B, S, CONV_WIDTH), bf16)],
        scratch_shapes=[pltpu.VMEM((8, CONV_WIDTH), jnp.float32)],
        compiler_params=pltpu.CompilerParams(
            dimension_semantics=("arbitrary", "arbitrary"),
            vmem_limit_bytes=VMEM_LIMIT),
        name="proj",
    )(x, posf, *consts)


def _attn_kernel(q_ref, k_ref, vt_ref, o_ref, s_scr, p_scr):
    S = q_ref.shape[2]
    tq = ATTN_TQ
    bf16 = jnp.bfloat16
    neg = jnp.finfo(jnp.float32).min
    key = lax.broadcasted_iota(jnp.int32, (tq, tq), 0)
    qry = lax.broadcasted_iota(jnp.int32, (tq, tq), 1)
    causal = key <= qry
    nt = S // tq
    tiles = list(range(0, nt, 2)) + list(range(nt - 1 - nt % 2, 0, -2))
    bodies = [(i, hh) for i in tiles for hh in range(HEADS_PER_STEP)]
    col_max = {}
    head_out = {}

    def chunk(c):
        return slice(c * tq, (c + 1) * tq)

    def scores(n):
        i, hh = bodies[n]
        hs = slice(hh * LANES, (hh + 1) * LANES)
        q = q_ref[0, 0, chunk(i), hs]
        part = None
        s_all = _dot_nt(k_ref[0, 0, 0:(i + 1) * tq, hs], q)
        for c in range(i + 1):
            s = s_all[chunk(c), :]
            if c == i:
                s = jnp.where(causal, s, neg)
            s_scr[n % 2, chunk(c), :] = s
            cm = jnp.max(s.reshape(tq // 8, 8, tq), axis=0)
            part = cm if part is None else jnp.maximum(part, cm)
            if c == i:
                col_max[n] = jnp.max(part, axis=0, keepdims=True)
            yield

    def exps(n):
        i, _ = bodies[n]
        m = col_max.pop(n)
        for c in range(i + 1):
            p_scr[n % 2, chunk(c), :] = jnp.exp2(s_scr[n % 2, chunk(c), :] - m).astype(bf16)
            yield

    def values(n):
        i, hh = bodies[n]
        acc = None
        for c in range(i + 1):
            vt = vt_ref[0, hh * V_ROWS:(hh + 1) * V_ROWS, chunk(c)]
            d = _dot(vt, p_scr[n % 2, chunk(c), :])
            acc = d if acc is None else acc + d
            if c == i:
                head_out[hh] = acc[0:V_DIM, :] * (1.0 / acc[V_DIM:V_DIM + 1, :])
                if hh == HEADS_PER_STEP - 1:
                    o_ref[0, 0, chunk(i), :] = jnp.concatenate(
                        [head_out.pop(h) for h in range(HEADS_PER_STEP)], axis=0).T
            yield

    for step in range(len(bodies) + 2):
        live = []
        if step < len(bodies):
            live.append(scores(step))
        if 0 <= step - 1 < len(bodies):
            live.append(exps(step - 1))
        if 0 <= step - 2 < len(bodies):
            live.append(values(step - 2))
        while live:
            live = [g for g in live if next(g, _DONE) is not _DONE]


def _attn_call(q, k, vt):
    B, _, S, _ = q.shape
    grid = (B, N_PAIRS)
    qk_spec = pl.BlockSpec((1, 1, S, PAIR_LANES), lambda b, h: (b, h, 0, 0))
    vt_spec = pl.BlockSpec((1, HEADS_PER_STEP * V_ROWS, S), lambda b, h: (b, h, 0))
    o_spec = pl.BlockSpec((1, 1, S, HEADS_PER_STEP * V_DIM), lambda b, h: (b, h, 0, 0))
    return pl.pallas_call(
        _attn_kernel,
        grid=grid,
        in_specs=[qk_spec, qk_spec, vt_spec],
        out_specs=o_spec,
        out_shape=jax.ShapeDtypeStruct((B, N_PAIRS, S, HEADS_PER_STEP * V_DIM), jnp.float32),
        scratch_shapes=[pltpu.VMEM((2, S, ATTN_TQ), jnp.float32),
                        pltpu.VMEM((2, S, ATTN_TQ), jnp.bfloat16)],
        compiler_params=pltpu.CompilerParams(
            dimension_semantics=("arbitrary", "arbitrary"),
            vmem_limit_bytes=VMEM_LIMIT),
        name="attn",
    )(q, k, vt)


def _post_kernel(x_ref, a_ref, c_ref, p_ref, an_ref, wo_ref, mpn_ref, fpre_ref,
                 wg_ref, wu_ref, wd_ref, fpost_ref, wpp_ref, pn_ref, wpg_ref,
                 o_ref):
    bf16 = jnp.bfloat16
    a = jnp.concatenate([a_ref[0, j] for j in range(N_PAIRS)], axis=-1)
    a = _rms(a, an_ref[...]).astype(bf16)
    m = jnp.concatenate([a, c_ref[...]], axis=-1)
    x1 = x_ref[...] + _rms(_dot(m, wo_ref[...]), mpn_ref[...])

    h = _rms(x1, fpre_ref[...]).astype(bf16)
    f = None
    for j in range(D_FF // FF_CHUNK):
        cs = slice(j * FF_CHUNK, (j + 1) * FF_CHUNK)
        g = _dot(h, wg_ref[:, cs])
        u = _dot(h, wu_ref[:, cs])
        act = (g * jax.nn.sigmoid(g) * u).astype(bf16)
        part = _dot(act, wd_ref[cs, :])
        f = part if f is None else f + part
    x2 = x1 + _rms(f, fpost_ref[...])

    e = _rms(_dot(p_ref[...].astype(bf16), wpp_ref[...]), pn_ref[...])
    gate = jax.nn.sigmoid(_dot(x2.astype(bf16), wpg_ref[...]))
    o_ref[...] = x2 + gate * e


def _post_call(x2d, a4d, c2d, p2d, an, wo, mpn, fpre, wg, wu, wd, fpost, wpp, pn, wpg):
    T = x2d.shape[0]
    tm = POST_TM
    tiles_per_row = a4d.shape[2] // tm
    tok = lambda w: pl.BlockSpec((tm, w), lambda t: (t, 0))
    a_spec = pl.BlockSpec((1, N_PAIRS, tm, a4d.shape[3]),
                          lambda t: (t // tiles_per_row, 0, t % tiles_per_row, 0))
    consts = (an, wo, mpn, fpre, wg, wu, wd, fpost, wpp, pn, wpg)
    return pl.pallas_call(
        _post_kernel,
        grid=(T // tm,),
        in_specs=[tok(D_MODEL), a_spec, tok(CONV_WIDTH), tok(PLE_DIM)]
        + [_const_spec(c.shape) for c in consts],
        out_specs=tok(D_MODEL),
        out_shape=jax.ShapeDtypeStruct((T, D_MODEL), jnp.float32),
        compiler_params=pltpu.CompilerParams(
            dimension_semantics=("arbitrary",),
            vmem_limit_bytes=VMEM_LIMIT),
        name="post",
    )(x2d, a4d, c2d, p2d, *consts)


def _inv_freq_column():
    inv_freq = 1.0 / (ROPE_THETA ** (np.arange(HALF_ROPE, dtype=np.float32) / HALF_ROPE))
    return jnp.asarray(inv_freq.astype(np.float32).reshape(HALF_ROPE, 1))


def _head_lanes(nope, rope):
    ref = nope if nope is not None else rope
    lead = ref.shape[:-1]
    zeros = lambda n: jnp.zeros(lead + (n,), ref.dtype)
    n_lo = HALF_LANES - HALF_ROPE
    parts = [
        rope[..., :HALF_ROPE] if rope is not None else zeros(HALF_ROPE),
        nope[..., :n_lo] if nope is not None else zeros(n_lo),
        rope[..., HALF_ROPE:] if rope is not None else zeros(HALF_ROPE),
        nope[..., n_lo:] if nope is not None else zeros(NOPE_DIM - n_lo),
        zeros(LANES - NOPE_DIM - ROPE_DIM),
    ]
    return jnp.concatenate(parts, axis=-1)


def _prep_w_in(w_in):
    o2 = Q_RANK + KV_RANK
    o3 = o2 + ROPE_DIM
    kr = _head_lanes(None, w_in[:, o2:o3])
    return jnp.concatenate([w_in[:, :o2], kr, w_in[:, o3:]], axis=1).astype(jnp.bfloat16)


def _prep_w_uq(w_uq):
    w = w_uq.reshape(Q_RANK, N_HEADS, NOPE_DIM + ROPE_DIM)
    wq = _head_lanes(w[..., :NOPE_DIM], w[..., NOPE_DIM:])
    return wq.reshape(Q_RANK, QK_WIDTH).astype(jnp.bfloat16)


def _prep_w_ukv(w_ukv):
    w = w_ukv.reshape(KV_RANK, N_HEADS, NOPE_DIM + V_DIM)
    k_nope, v = w[..., :NOPE_DIM], w[..., NOPE_DIM:]
    wk = _head_lanes(k_nope, None).reshape(KV_RANK, QK_WIDTH)
    v_pad = jnp.concatenate([v, jnp.zeros((KV_RANK, N_HEADS, SUM_ROWS), w.dtype)], axis=-1)
    wv_t = v_pad.reshape(KV_RANK, VT_WIDTH).T
    return wk.astype(jnp.bfloat16), wv_t.astype(jnp.bfloat16)


def _ones_rows_column():
    col = np.zeros((N_HEADS, V_ROWS, 1), np.float32)
    col[:, V_DIM:, :] = 1.0
    return jnp.asarray(col.reshape(VT_WIDTH, 1))


def kernel(x, p, positions, mix_pre_norm, w_in, q_norm, kv_norm, w_uq, w_ukv, conv_w, attn_group_norm, conv_group_norm, w_out, mix_post_norm, ffn_pre_norm, w_gate, w_up, w_down, ffn_post_norm, w_ple_proj, ple_norm, w_ple_gate):
    B, S, D = x.shape
    depth = w_in.shape[0]
    bf16 = jnp.bfloat16
    row = lambda g: g.reshape(1, -1)
    invf = _inv_freq_column()
    vone = _ones_rows_column()
    posf = positions.astype(jnp.float32)[:, None, :]
    for i in range(depth):
        wk, wv = _prep_w_ukv(w_ukv[i])
        q, k, v, c = _proj_call(
            x, posf, row(mix_pre_norm[i]), _prep_w_in(w_in[i]), row(q_norm[i]),
            row(kv_norm[i]), _prep_w_uq(w_uq[i]), wk, wv, vone, conv_w[i],
            row(conv_group_norm[i]), invf)
        a = _attn_call(q, k, v)
        out = _post_call(
            x.reshape(B * S, D), a,
            c.reshape(B * S, CONV_WIDTH), p[i].reshape(B * S, PLE_DIM),
            row(attn_group_norm[i]), w_out[i].astype(bf16), row(mix_post_norm[i]),
            row(ffn_pre_norm[i]), w_gate[i].astype(bf16), w_up[i].astype(bf16),
            w_down[i].astype(bf16), row(ffn_post_norm[i]),
            w_ple_proj[i].astype(bf16), row(ple_norm[i]), w_ple_gate[i].astype(bf16))
        x = out.reshape(B, S, D)
    return x
```

```python
import functools

import numpy as np
import jax
import jax.numpy as jnp
from jax import lax
from jax.experimental import pallas as pl
from jax.experimental.pallas import tpu as pltpu

D_MODEL = 1024
PLE_DIM = 256
ATTN_WIDTH = 512
CONV_WIDTH = 512
V_DIM = 64
NOPE_DIM = 64
ROPE_DIM = 32
HALF_ROPE = ROPE_DIM // 2
N_HEADS = 8
Q_RANK = 256
KV_RANK = 256
CONV_K = 3
D_FF = 2816
ROPE_THETA = 10000.0
EPS = 1e-6
LOG2_E = 1.4426950408889634

LANES = 128
HALF_LANES = LANES // 2
QK_WIDTH = N_HEADS * LANES
HEADS_PER_STEP = 2
N_PAIRS = N_HEADS // HEADS_PER_STEP
PAIR_LANES = HEADS_PER_STEP * LANES
SUM_ROWS = 16
V_ROWS = V_DIM + SUM_ROWS
VT_WIDTH = N_HEADS * V_ROWS
FF_CHUNK = 256

PROJ_TM = 1024
PROJ_ROW_GROUPS = 2
ATTN_TQ = 256
POST_TM = 512
VMEM_LIMIT = 56 * 1024 * 1024

_O_CQ = 0
_O_CKV = _O_CQ + Q_RANK
_O_KR = _O_CKV + KV_RANK
_O_BG = _O_KR + LANES
_O_CG = _O_BG + CONV_WIDTH
_O_XV = _O_CG + CONV_WIDTH
IN_EXT = _O_XV + CONV_WIDTH


_DONE = object()


def _rms(x, g):
    ms = jnp.mean(x * x, axis=-1, keepdims=True)
    return x * lax.rsqrt(ms + EPS) * g


def _dot(a, b):
    return jnp.dot(a, b, preferred_element_type=jnp.float32)


def _dependent_zero(x):
    bits = lax.bitcast_convert_type(x, jnp.int32)
    bits = lax.shift_right_logical(lax.shift_right_logical(bits, 16), 16)
    return bits.astype(jnp.float32)


def _dot_nt(a, b):
    return lax.dot_general(a, b, (((1,), (1,)), ((), ())),
                           preferred_element_type=jnp.float32)


def _proj_kernel(x_ref, pos_ref, gpre_ref, win_ref, qn_ref, kvn_ref, wq_ref,
                 wk_ref, wv_ref, vone_ref, cw_ref, cn_ref, invf_ref,
                 q_ref, k_ref, v_ref, c_ref, utail_ref):
    tm = x_ref.shape[1]
    bf16 = jnp.bfloat16
    rg = tm // PROJ_ROW_GROUPS

    @pl.when(pl.program_id(1) == 0)
    def _():
        utail_ref[...] = jnp.zeros_like(utail_ref)

    def rows_pipeline(rows):
        h = _rms(x_ref[0, rows, :], gpre_ref[...]).astype(bf16)
        yield
        z = _dot(h, win_ref[...])
        yield
        cq = _rms(z[:, _O_CQ:_O_CQ + Q_RANK], qn_ref[...]).astype(bf16)
        qq = _dot(cq, wq_ref[...])
        ckv = _rms(z[:, _O_CKV:_O_CKV + KV_RANK], kvn_ref[...]).astype(bf16)
        kk = _dot(ckv, wk_ref[...])
        v_ref[0, :, rows] = (_dot_nt(wv_ref[...], ckv) + vone_ref[...]).astype(bf16)

        ang_t = invf_ref[...] * pos_ref[0, :, rows]
        cos_t = jnp.cos(ang_t)
        sin_t = jnp.sin(ang_t)
        one_t = jnp.ones((HALF_LANES - HALF_ROPE, rg), jnp.float32)
        zero_t = jnp.zeros((HALF_LANES - HALF_ROPE, rg), jnp.float32)
        cos = jnp.concatenate([cos_t, one_t, cos_t, one_t], axis=0).T
        sin = jnp.concatenate([-sin_t, zero_t, sin_t, zero_t], axis=0).T
        scale = (NOPE_DIM + ROPE_DIM) ** -0.5 * LOG2_E
        cos_q = cos * scale
        sin_q = sin * scale
        kr = z[:, _O_KR:_O_KR + LANES]
        kr = kr * cos + pltpu.roll(kr, HALF_LANES, axis=1) * sin

        u = z[:, _O_CG:_O_CG + CONV_WIDTH] * z[:, _O_XV:_O_XV + CONV_WIDTH]
        tail = utail_ref[...]
        row = lax.broadcasted_iota(jnp.int32, (rg, CONV_WIDTH), 0)
        u1 = pltpu.roll(u, 1, axis=0)
        u1 = jnp.where(row == 0, tail[7:8, :], u1)
        u2 = pltpu.roll(u, 2, axis=0)
        u2 = jnp.where(row == 0, tail[6:7, :], jnp.where(row == 1, tail[7:8, :], u2))
        utail_ref[...] = u[rg - 8:rg, :]
        cw = cw_ref[...]
        y = cw[0:1, :] * u + cw[1:2, :] * u1 + cw[2:3, :] * u2
        cc = z[:, _O_BG:_O_BG + CONV_WIDTH] * y
        c_ref[0, rows, :] = _rms(cc, cn_ref[...]).astype(bf16)
        yield

        for hh in range(N_HEADS):
            lo = hh * LANES
            qh = qq[:, lo:lo + LANES]
            qh = qh * cos_q + pltpu.roll(qh, HALF_LANES, axis=1) * sin_q
            pair, ps = divmod(hh, HEADS_PER_STEP)
            pl_ = slice(ps * LANES, (ps + 1) * LANES)
            q_ref[0, pair, rows, pl_] = qh.astype(bf16)
            k_ref[0, pair, rows, pl_] = (kk[:, lo:lo + LANES] + kr).astype(bf16)

    live = []
    for r in range(PROJ_ROW_GROUPS):
        live.insert(0, rows_pipeline(slice(r * rg, (r + 1) * rg)))
        for _ in range(2):
            live = [g for g in live if next(g, _DONE) is not _DONE]
    while live:
        live = [g for g in live if next(g, _DONE) is not _DONE]


def _const_spec(shape):
    nd = len(shape)
    return pl.BlockSpec(shape, lambda *_: (0,) * nd, pipeline_mode=pl.Buffered(1))


def _proj_call(x, posf, gpre, win, qn, kvn, wq, wk, wv, vone, cw, cn, invf):
    B, S, _ = x.shape
    tm = PROJ_TM
    grid = (B, S // tm)
    tok = lambda w: pl.BlockSpec((1, tm, w), lambda b, s: (b, s, 0))
    pos_spec = pl.BlockSpec((1, 1, tm), lambda b, s: (b, 0, s))
    qk_spec = pl.BlockSpec((1, N_PAIRS, tm, PAIR_LANES), lambda b, s: (b, 0, s, 0))
    consts = (gpre, win, qn, kvn, wq, wk, wv, vone, cw, cn, invf)
    bf16 = jnp.bfloat16
    return pl.pallas_call(
        _proj_kernel,
        grid=grid,
        in_specs=[tok(D_MODEL), pos_spec] + [_const_spec(c.shape) for c in consts],
        out_specs=[qk_spec, qk_spec,
                   pl.BlockSpec((1, VT_WIDTH, tm), lambda b, s: (b, 0, s)),
                   tok(CONV_WIDTH)],
        out_shape=[jax.ShapeDtypeStruct((B, N_PAIRS, S, PAIR_LANES), bf16),
                   jax.ShapeDtypeStruct((B, N_PAIRS, S, PAIR_LANES), bf16),
                   jax.ShapeDtypeStruct((B, VT_WIDTH, S), bf16),
                   jax.ShapeDtypeStruct((B, S, CONV_WIDTH), bf16)],
        scratch_shapes=[pltpu.VMEM((8, CONV_WIDTH), jnp.float32)],
        compiler_params=pltpu.CompilerParams(
            dimension_semantics=("arbitrary", "arbitrary"),
            vmem_limit_bytes=VMEM_LIMIT),
        name="proj",
    )(x, posf, *consts)


def _attn_kernel(q_ref, k_ref, vt_ref, o_ref, s_scr, p_scr):
    S = q_ref.shape[2]
    tq = ATTN_TQ
    bf16 = jnp.bfloat16
    neg = jnp.finfo(jnp.float32).min
    key = lax.broadcasted_iota(jnp.int32, (tq, tq), 0)
    qry = lax.broadcasted_iota(jnp.int32, (tq, tq), 1)
    causal = key <= qry
    nt = S // tq
    tiles = list(range(0, nt, 2)) + list(range(nt - 1 - nt % 2, 0, -2))
    bodies = [(i, hh) for i in tiles for hh in range(HEADS_PER_STEP)]
    col_max = {}
    head_out = {}
    pace = {"tok": jnp.zeros((8, tq), jnp.float32)}

    def chunk(c):
        return slice(c * tq, (c + 1) * tq)

    def scores(n):
        i, hh = bodies[n]
        hs = slice(hh * LANES, (hh + 1) * LANES)
        q = q_ref[0, 0, chunk(i), hs]
        part = None
        split = (i + 2) // 2
        s_lo = _dot_nt(k_ref[0, 0, 0:split * tq, hs], q)
        s_hi = _dot_nt(k_ref[0, 0, split * tq:(i + 1) * tq, hs], q) if split <= i else None
        order = [c for pair in zip(range(split), range(split, 2 * split)) for c in pair]
        order = [c for c in order if c <= i]
        for pos, c in enumerate(order):
            s = s_lo[chunk(c), :] if c < split else s_hi[chunk(c - split), :]
            if c == i:
                s = jnp.where(causal, s, neg)
            s_scr[n % 2, chunk(c), :] = s
            cm = jnp.max(s.reshape(tq // 8, 8, tq), axis=0)
            part = cm if part is None else jnp.maximum(part, cm)
            pace["tok"] = cm
            if pos == len(order) - 1:
                col_max[n] = jnp.max(part, axis=0, keepdims=True)
            yield

    def exps(n):
        i, _ = bodies[n]
        m = col_max.pop(n)
        for c in range(i + 1):
            m_c = m + _dependent_zero(pace["tok"])[0:1, :]
            p_scr[n % 2, chunk(c), :] = jnp.exp2((s_scr[n % 2, chunk(c), :] - m_c).astype(bf16))
            yield

    def values(n):
        i, hh = bodies[n]
        acc = None
        for c in range(i + 1):
            vt = vt_ref[0, hh * V_ROWS:(hh + 1) * V_ROWS, chunk(c)]
            d = _dot(vt, p_scr[n % 2, chunk(c), :])
            acc = d if acc is None else acc + d
            if c == i:
                head_out[hh] = acc[0:V_DIM, :] * (1.0 / acc[V_DIM:V_DIM + 1, :])
                if hh == HEADS_PER_STEP - 1:
                    o_ref[0, 0, chunk(i), :] = jnp.concatenate(
                        [head_out.pop(h) for h in range(HEADS_PER_STEP)], axis=0).T
            yield

    for step in range(len(bodies) + 2):
        live = []
        if step < len(bodies):
            live.append(scores(step))
        if 0 <= step - 1 < len(bodies):
            live.append(exps(step - 1))
        if 0 <= step - 2 < len(bodies):
            live.append(values(step - 2))
        while live:
            live = [g for g in live if next(g, _DONE) is not _DONE]


def _attn_call(q, k, vt):
    B, _, S, _ = q.shape
    grid = (B, N_PAIRS)
    qk_spec = pl.BlockSpec((1, 1, S, PAIR_LANES), lambda b, h: (b, h, 0, 0))
    vt_spec = pl.BlockSpec((1, HEADS_PER_STEP * V_ROWS, S), lambda b, h: (b, h, 0))
    o_spec = pl.BlockSpec((1, 1, S, HEADS_PER_STEP * V_DIM), lambda b, h: (b, h, 0, 0))
    return pl.pallas_call(
        _attn_kernel,
        grid=grid,
        in_specs=[qk_spec, qk_spec, vt_spec],
        out_specs=o_spec,
        out_shape=jax.ShapeDtypeStruct((B, N_PAIRS, S, HEADS_PER_STEP * V_DIM), jnp.float32),
        scratch_shapes=[pltpu.VMEM((2, S, ATTN_TQ), jnp.float32),
                        pltpu.VMEM((2, S, ATTN_TQ), jnp.bfloat16)],
        compiler_params=pltpu.CompilerParams(
            dimension_semantics=("arbitrary", "arbitrary"),
            vmem_limit_bytes=VMEM_LIMIT),
        name="attn",
    )(q, k, vt)


def _post_kernel(x_ref, a_ref, c_ref, p_ref, an_ref, wo_ref, mpn_ref, fpre_ref,
                 wg_ref, wu_ref, wd_ref, fpost_ref, wpp_ref, pn_ref, wpg_ref,
                 o_ref):
    bf16 = jnp.bfloat16
    a = jnp.concatenate([a_ref[0, j] for j in range(N_PAIRS)], axis=-1)
    a = _rms(a, an_ref[...]).astype(bf16)
    m = jnp.concatenate([a, c_ref[...]], axis=-1)
    x1 = x_ref[...] + _rms(_dot(m, wo_ref[...]), mpn_ref[...])

    h = _rms(x1, fpre_ref[...]).astype(bf16)
    f = None
    for j in range(D_FF // FF_CHUNK):
        cs = slice(j * FF_CHUNK, (j + 1) * FF_CHUNK)
        g = _dot(h, wg_ref[:, cs])
        u = _dot(h, wu_ref[:, cs])
        act = (g * jax.nn.sigmoid(g) * u).astype(bf16)
        part = _dot(act, wd_ref[cs, :])
        f = part if f is None else f + part
    x2 = x1 + _rms(f, fpost_ref[...])

    e = _rms(_dot(p_ref[...].astype(bf16), wpp_ref[...]), pn_ref[...])
    gate = jax.nn.sigmoid(_dot(x2.astype(bf16), wpg_ref[...]))
    o_ref[...] = x2 + gate * e


def _post_call(x2d, a4d, c2d, p2d, an, wo, mpn, fpre, wg, wu, wd, fpost, wpp, pn, wpg):
    T = x2d.shape[0]
    tm = POST_TM
    tiles_per_row = a4d.shape[2] // tm
    tok = lambda w: pl.BlockSpec((tm, w), lambda t: (t, 0))
    a_spec = pl.BlockSpec((1, N_PAIRS, tm, a4d.shape[3]),
                          lambda t: (t // tiles_per_row, 0, t % tiles_per_row, 0))
    consts = (an, wo, mpn, fpre, wg, wu, wd, fpost, wpp, pn, wpg)
    return pl.pallas_call(
        _post_kernel,
        grid=(T // tm,),
        in_specs=[tok(D_MODEL), a_spec, tok(CONV_WIDTH), tok(PLE_DIM)]
        + [_const_spec(c.shape) for c in consts],
        out_specs=tok(D_MODEL),
        out_shape=jax.ShapeDtypeStruct((T, D_MODEL), jnp.float32),
        compiler_params=pltpu.CompilerParams(
            dimension_semantics=("arbitrary",),
            vmem_limit_bytes=VMEM_LIMIT),
        name="post",
    )(x2d, a4d, c2d, p2d, *consts)


def _inv_freq_column():
    inv_freq = 1.0 / (ROPE_THETA ** (np.arange(HALF_ROPE, dtype=np.float32) / HALF_ROPE))
    return jnp.asarray(inv_freq.astype(np.float32).reshape(HALF_ROPE, 1))


def _head_lanes(nope, rope):
    ref = nope if nope is not None else rope
    lead = ref.shape[:-1]
    zeros = lambda n: jnp.zeros(lead + (n,), ref.dtype)
    n_lo = HALF_LANES - HALF_ROPE
    parts = [
        rope[..., :HALF_ROPE] if rope is not None else zeros(HALF_ROPE),
        nope[..., :n_lo] if nope is not None else zeros(n_lo),
        rope[..., HALF_ROPE:] if rope is not None else zeros(HALF_ROPE),
        nope[..., n_lo:] if nope is not None else zeros(NOPE_DIM - n_lo),
        zeros(LANES - NOPE_DIM - ROPE_DIM),
    ]
    return jnp.concatenate(parts, axis=-1)


def _prep_w_in(w_in):
    o2 = Q_RANK + KV_RANK
    o3 = o2 + ROPE_DIM
    kr = _head_lanes(None, w_in[:, o2:o3])
    return jnp.concatenate([w_in[:, :o2], kr, w_in[:, o3:]], axis=1).astype(jnp.bfloat16)


def _prep_w_uq(w_uq):
    w = w_uq.reshape(Q_RANK, N_HEADS, NOPE_DIM + ROPE_DIM)
    wq = _head_lanes(w[..., :NOPE_DIM], w[..., NOPE_DIM:])
    return wq.reshape(Q_RANK, QK_WIDTH).astype(jnp.bfloat16)


def _prep_w_ukv(w_ukv):
    w = w_ukv.reshape(KV_RANK, N_HEADS, NOPE_DIM + V_DIM)
    k_nope, v = w[..., :NOPE_DIM], w[..., NOPE_DIM:]
    wk = _head_lanes(k_nope, None).reshape(KV_RANK, QK_WIDTH)
    v_pad = jnp.concatenate([v, jnp.zeros((KV_RANK, N_HEADS, SUM_ROWS), w.dtype)], axis=-1)
    wv_t = v_pad.reshape(KV_RANK, VT_WIDTH).T
    return wk.astype(jnp.bfloat16), wv_t.astype(jnp.bfloat16)


def _ones_rows_column():
    col = np.zeros((N_HEADS, V_ROWS, 1), np.float32)
    col[:, V_DIM:, :] = 1.0
    return jnp.asarray(col.reshape(VT_WIDTH, 1))


def kernel(x, p, positions, mix_pre_norm, w_in, q_norm, kv_norm, w_uq, w_ukv, conv_w, attn_group_norm, conv_group_norm, w_out, mix_post_norm, ffn_pre_norm, w_gate, w_up, w_down, ffn_post_norm, w_ple_proj, ple_norm, w_ple_gate):
    B, S, D = x.shape
    depth = w_in.shape[0]
    bf16 = jnp.bfloat16
    row = lambda g: g.reshape(1, -1)
    invf = _inv_freq_column()
    vone = _ones_rows_column()
    posf = positions.astype(jnp.float32)[:, None, :]
    for i in range(depth):
        wk, wv = _prep_w_ukv(w_ukv[i])
        q, k, v, c = _proj_call(
            x, posf, row(mix_pre_norm[i]), _prep_w_in(w_in[i]), row(q_norm[i]),
            row(kv_norm[i]), _prep_w_uq(w_uq[i]), wk, wv, vone, conv_w[i],
            row(conv_group_norm[i]), invf)
        a = _attn_call(q, k, v)
        out = _post_call(
            x.reshape(B * S, D), a,
            c.reshape(B * S, CONV_WIDTH), p[i].reshape(B * S, PLE_DIM),
            row(attn_group_norm[i]), w_out[i].astype(bf16), row(mix_post_norm[i]),
            row(ffn_pre_norm[i]), w_gate[i].astype(bf16), w_up[i].astype(bf16),
            w_down[i].astype(bf16), row(ffn_post_norm[i]),
            w_ple_proj[i].astype(bf16), row(ple_norm[i]), w_ple_gate[i].astype(bf16))
        x = out.reshape(B, S, D)
    return x
```

```python
import numpy as np
import jax
import jax.numpy as jnp
from jax import lax
from jax.experimental import pallas as pl
from jax.experimental.pallas import tpu as pltpu

D_MODEL = 1024
PLE_DIM = 256
ATTN_WIDTH = 512
CONV_WIDTH = 512
V_DIM = 64
NOPE_DIM = 64
ROPE_DIM = 32
HALF_ROPE = ROPE_DIM // 2
N_HEADS = 8
Q_RANK = 256
KV_RANK = 256
CONV_K = 3
D_FF = 2816
ROPE_THETA = 10000.0
EPS = 1e-6
LOG2_E = 1.4426950408889634

LANES = 128
HALF_LANES = LANES // 2
QK_WIDTH = N_HEADS * LANES
HEADS_PER_STEP = 2
N_PAIRS = N_HEADS // HEADS_PER_STEP
PAIR_LANES = HEADS_PER_STEP * LANES
SUM_ROWS = 16
V_ROWS = V_DIM + SUM_ROWS
VT_WIDTH = N_HEADS * V_ROWS
FF_CHUNK = 256

PROJ_TM = 1024
PROJ_ROW_GROUPS = 2
ATTN_TQ = 256
QK_STREAMS = 4
POST_TM = 512
VMEM_LIMIT = 56 * 1024 * 1024

_O_CQ = 0
_O_CKV = _O_CQ + Q_RANK
_O_KR = _O_CKV + KV_RANK
_O_BG = _O_KR + LANES
_O_CG = _O_BG + CONV_WIDTH
_O_XV = _O_CG + CONV_WIDTH
IN_EXT = _O_XV + CONV_WIDTH


_DONE = object()


def _rms(x, g):
    ms = jnp.mean(x * x, axis=-1, keepdims=True)
    return x * lax.rsqrt(ms + EPS) * g


def _dot(a, b):
    return jnp.dot(a, b, preferred_element_type=jnp.float32)


def _dependent_zero(x):
    bits = lax.bitcast_convert_type(x, jnp.int32)
    bits = lax.shift_right_logical(lax.shift_right_logical(bits, 16), 16)
    return bits.astype(jnp.float32)


def _dot_nt(a, b):
    return lax.dot_general(a, b, (((1,), (1,)), ((), ())),
                           preferred_element_type=jnp.float32)


def _proj_kernel(x_ref, pos_ref, gpre_ref, win_ref, qn_ref, kvn_ref, wq_ref,
                 wk_ref, wv_ref, vone_ref, cw_ref, cn_ref, invf_ref,
                 q_ref, k_ref, v_ref, c_ref, utail_ref):
    tm = x_ref.shape[1]
    bf16 = jnp.bfloat16
    rg = tm // PROJ_ROW_GROUPS

    @pl.when(pl.program_id(1) == 0)
    def _():
        utail_ref[...] = jnp.zeros_like(utail_ref)

    def rows_pipeline(rows):
        h = _rms(x_ref[0, rows, :], gpre_ref[...]).astype(bf16)
        yield
        z = _dot(h, win_ref[...])
        yield
        cq = _rms(z[:, _O_CQ:_O_CQ + Q_RANK], qn_ref[...]).astype(bf16)
        qq = _dot(cq, wq_ref[...])
        ckv = _rms(z[:, _O_CKV:_O_CKV + KV_RANK], kvn_ref[...]).astype(bf16)
        kk = _dot(ckv, wk_ref[...])
        v_ref[0, :, rows] = (_dot_nt(wv_ref[...], ckv) + vone_ref[...]).astype(bf16)

        ang_t = invf_ref[...] * pos_ref[0, :, rows]
        cos_t = jnp.cos(ang_t)
        sin_t = jnp.sin(ang_t)
        one_t = jnp.ones((HALF_LANES - HALF_ROPE, rg), jnp.float32)
        zero_t = jnp.zeros((HALF_LANES - HALF_ROPE, rg), jnp.float32)
        cos = jnp.concatenate([cos_t, one_t, cos_t, one_t], axis=0).T
        sin = jnp.concatenate([-sin_t, zero_t, sin_t, zero_t], axis=0).T
        scale = (NOPE_DIM + ROPE_DIM) ** -0.5 * LOG2_E
        cos_q = cos * scale
        sin_q = sin * scale
        kr = z[:, _O_KR:_O_KR + LANES]
        kr = kr * cos + pltpu.roll(kr, HALF_LANES, axis=1) * sin

        u = z[:, _O_CG:_O_CG + CONV_WIDTH] * z[:, _O_XV:_O_XV + CONV_WIDTH]
        tail = utail_ref[...]
        row = lax.broadcasted_iota(jnp.int32, (rg, CONV_WIDTH), 0)
        u1 = pltpu.roll(u, 1, axis=0)
        u1 = jnp.where(row == 0, tail[7:8, :], u1)
        u2 = pltpu.roll(u, 2, axis=0)
        u2 = jnp.where(row == 0, tail[6:7, :], jnp.where(row == 1, tail[7:8, :], u2))
        utail_ref[...] = u[rg - 8:rg, :]
        cw = cw_ref[...]
        y = cw[0:1, :] * u + cw[1:2, :] * u1 + cw[2:3, :] * u2
        cc = z[:, _O_BG:_O_BG + CONV_WIDTH] * y
        c_ref[0, rows, :] = _rms(cc, cn_ref[...]).astype(bf16)
        yield

        for hh in range(N_HEADS):
            lo = hh * LANES
            qh = qq[:, lo:lo + LANES]
            qh = qh * cos_q + pltpu.roll(qh, HALF_LANES, axis=1) * sin_q
            pair, ps = divmod(hh, HEADS_PER_STEP)
            pl_ = slice(ps * LANES, (ps + 1) * LANES)
            q_ref[0, pair, rows, pl_] = qh.astype(bf16)
            k_ref[0, pair, rows, pl_] = (kk[:, lo:lo + LANES] + kr).astype(bf16)

    live = []
    for r in range(PROJ_ROW_GROUPS):
        live.insert(0, rows_pipeline(slice(r * rg, (r + 1) * rg)))
        for _ in range(2):
            live = [g for g in live if next(g, _DONE) is not _DONE]
    while live:
        live = [g for g in live if next(g, _DONE) is not _DONE]


def _const_spec(shape):
    nd = len(shape)
    return pl.BlockSpec(shape, lambda *_: (0,) * nd, pipeline_mode=pl.Buffered(1))


def _proj_call(x, posf, gpre, win, qn, kvn, wq, wk, wv, vone, cw, cn, invf):
    B, S, _ = x.shape
    tm = PROJ_TM
    grid = (B, S // tm)
    tok = lambda w: pl.BlockSpec((1, tm, w), lambda b, s: (b, s, 0))
    pos_spec = pl.BlockSpec((1, 1, tm), lambda b, s: (b, 0, s))
    qk_spec = pl.BlockSpec((1, N_PAIRS, tm, PAIR_LANES), lambda b, s: (b, 0, s, 0))
    consts = (gpre, win, qn, kvn, wq, wk, wv, vone, cw, cn, invf)
    bf16 = jnp.bfloat16
    return pl.pallas_call(
        _proj_kernel,
        grid=grid,
        in_specs=[tok(D_MODEL), pos_spec] + [_const_spec(c.shape) for c in consts],
        out_specs=[qk_spec, qk_spec,
                   pl.BlockSpec((1, VT_WIDTH, tm), lambda b, s: (b, 0, s)),
                   tok(CONV_WIDTH)],
        out_shape=[jax.ShapeDtypeStruct((B, N_PAIRS, S, PAIR_LANES), bf16),
                   jax.ShapeDtypeStruct((B, N_PAIRS, S, PAIR_LANES), bf16),
                   jax.ShapeDtypeStruct((B, VT_WIDTH, S), bf16),
                   jax.ShapeDtypeStruct((B, S, CONV_WIDTH), bf16)],
        scratch_shapes=[pltpu.VMEM((8, CONV_WIDTH), jnp.float32)],
        compiler_params=pltpu.CompilerParams(
            dimension_semantics=("arbitrary", "arbitrary"),
            vmem_limit_bytes=VMEM_LIMIT),
        name="proj",
    )(x, posf, *consts)


def _attn_kernel(q_ref, k_ref, vt_ref, o_ref, s_scr, p_scr):
    S = q_ref.shape[2]
    tq = ATTN_TQ
    bf16 = jnp.bfloat16
    neg = jnp.finfo(jnp.float32).min
    key = lax.broadcasted_iota(jnp.int32, (tq, tq), 0)
    qry = lax.broadcasted_iota(jnp.int32, (tq, tq), 1)
    causal = key <= qry
    nt = S // tq
    tiles = list(range(0, nt, 2)) + list(range(nt - 1 - nt % 2, 0, -2))
    bodies = [(i, hh) for i in tiles for hh in range(HEADS_PER_STEP)]

    def chunk(c):
        return slice(c * tq, (c + 1) * tq)

    def one_pair(pair, carry):
        col_max = {}
        head_out = {}
        pace = {"tok": jnp.zeros((8, tq), jnp.float32)}

        def scores(n):
            i, hh = bodies[n]
            hs = slice(hh * LANES, (hh + 1) * LANES)
            q = q_ref[0, pair, chunk(i), hs]
            part = None
            per = -(-(i + 1) // QK_STREAMS)
            starts = list(range(0, i + 1, per))
            streams = [_dot_nt(k_ref[0, pair, c0 * tq:min(c0 + per, i + 1) * tq, hs], q)
                       for c0 in starts]
            order = [c0 + j for j in range(per) for c0 in starts
                     if c0 + j <= min(c0 + per - 1, i)]
            for pos, c in enumerate(order):
                s = streams[c // per][chunk(c % per), :]
                if c == i:
                    s = jnp.where(causal, s, neg)
                s_scr[n % 2, chunk(c), :] = s
                cm = jnp.max(s.reshape(tq // 8, 8, tq), axis=0)
                part = cm if part is None else jnp.maximum(part, cm)
                pace["tok"] = cm
                if pos == len(order) - 1:
                    col_max[n] = jnp.max(part, axis=0, keepdims=True)
                yield

        def exps(n):
            i, _ = bodies[n]
            m = col_max.pop(n)
            for c in range(i + 1):
                m_c = m + _dependent_zero(pace["tok"])[0:1, :]
                p_scr[n % 2, chunk(c), :] = jnp.exp2(
                    (s_scr[n % 2, chunk(c), :] - m_c).astype(bf16))
                yield

        def values(n):
            i, hh = bodies[n]
            acc = None
            for c in range(i + 1):
                vt = vt_ref[0, pair, hh * V_ROWS:(hh + 1) * V_ROWS, chunk(c)]
                d = _dot(vt, p_scr[n % 2, chunk(c), :])
                acc = d if acc is None else acc + d
                if c == i:
                    head_out[hh] = acc[0:V_DIM, :] * (1.0 / acc[V_DIM:V_DIM + 1, :])
                    if hh == HEADS_PER_STEP - 1:
                        o_ref[0, pair, chunk(i), :] = jnp.concatenate(
                            [head_out.pop(h) for h in range(HEADS_PER_STEP)], axis=0).T
                yield

        for step in range(len(bodies) + 2):
            live = []
            if step < len(bodies):
                live.append(scores(step))
            if 0 <= step - 1 < len(bodies):
                live.append(exps(step - 1))
            if 0 <= step - 2 < len(bodies):
                live.append(values(step - 2))
            while live:
                live = [g for g in live if next(g, _DONE) is not _DONE]
        return carry

    lax.fori_loop(0, q_ref.shape[1], one_pair, 0)


def _attn_call(q, k, vt):
    B, _, S, _ = q.shape
    vt = vt.reshape(B, N_PAIRS, HEADS_PER_STEP * V_ROWS, S)
    grid = (B,)
    qk_spec = pl.BlockSpec((1, N_PAIRS, S, PAIR_LANES), lambda b: (b, 0, 0, 0))
    vt_spec = pl.BlockSpec((1, N_PAIRS, HEADS_PER_STEP * V_ROWS, S), lambda b: (b, 0, 0, 0))
    o_spec = pl.BlockSpec((1, N_PAIRS, S, HEADS_PER_STEP * V_DIM), lambda b: (b, 0, 0, 0))
    return pl.pallas_call(
        _attn_kernel,
        grid=grid,
        in_specs=[qk_spec, qk_spec, vt_spec],
        out_specs=o_spec,
        out_shape=jax.ShapeDtypeStruct((B, N_PAIRS, S, HEADS_PER_STEP * V_DIM), jnp.float32),
        scratch_shapes=[pltpu.VMEM((2, S, ATTN_TQ), jnp.float32),
                        pltpu.VMEM((2, S, ATTN_TQ), jnp.bfloat16)],
        compiler_params=pltpu.CompilerParams(
            dimension_semantics=("arbitrary",),
            vmem_limit_bytes=VMEM_LIMIT),
        name="attn",
    )(q, k, vt)


def _post_kernel(x_ref, a_ref, c_ref, p_ref, an_ref, wo_ref, mpn_ref, fpre_ref,
                 wg_ref, wu_ref, wd_ref, fpost_ref, wpp_ref, pn_ref, wpg_ref,
                 o_ref):
    bf16 = jnp.bfloat16
    a = jnp.concatenate([a_ref[0, j] for j in range(N_PAIRS)], axis=-1)
    a = _rms(a, an_ref[...]).astype(bf16)
    m = jnp.concatenate([a, c_ref[...]], axis=-1)
    x1 = x_ref[...] + _rms(_dot(m, wo_ref[...]), mpn_ref[...])

    h = _rms(x1, fpre_ref[...]).astype(bf16)
    f = None
    for j in range(D_FF // FF_CHUNK):
        cs = slice(j * FF_CHUNK, (j + 1) * FF_CHUNK)
        g = _dot(h, wg_ref[:, cs])
        u = _dot(h, wu_ref[:, cs])
        act = (g * jax.nn.sigmoid(g) * u).astype(bf16)
        part = _dot(act, wd_ref[cs, :])
        f = part if f is None else f + part
    x2 = x1 + _rms(f, fpost_ref[...])

    e = _rms(_dot(p_ref[...].astype(bf16), wpp_ref[...]), pn_ref[...])
    gate = jax.nn.sigmoid(_dot(x2.astype(bf16), wpg_ref[...]))
    o_ref[...] = x2 + gate * e


def _post_call(x2d, a4d, c2d, p2d, an, wo, mpn, fpre, wg, wu, wd, fpost, wpp, pn, wpg):
    T = x2d.shape[0]
    tm = POST_TM
    tiles_per_row = a4d.shape[2] // tm
    tok = lambda w: pl.BlockSpec((tm, w), lambda t: (t, 0))
    a_spec = pl.BlockSpec((1, N_PAIRS, tm, a4d.shape[3]),
                          lambda t: (t // tiles_per_row, 0, t % tiles_per_row, 0))
    consts = (an, wo, mpn, fpre, wg, wu, wd, fpost, wpp, pn, wpg)
    return pl.pallas_call(
        _post_kernel,
        grid=(T // tm,),
        in_specs=[tok(D_MODEL), a_spec, tok(CONV_WIDTH), tok(PLE_DIM)]
        + [_const_spec(c.shape) for c in consts],
        out_specs=tok(D_MODEL),
        out_shape=jax.ShapeDtypeStruct((T, D_MODEL), jnp.float32),
        compiler_params=pltpu.CompilerParams(
            dimension_semantics=("arbitrary",),
            vmem_limit_bytes=VMEM_LIMIT),
        name="post",
    )(x2d, a4d, c2d, p2d, *consts)


def _inv_freq_column():
    inv_freq = 1.0 / (ROPE_THETA ** (np.arange(HALF_ROPE, dtype=np.float32) / HALF_ROPE))
    return jnp.asarray(inv_freq.astype(np.float32).reshape(HALF_ROPE, 1))


def _head_lanes(nope, rope):
    ref = nope if nope is not None else rope
    lead = ref.shape[:-1]
    zeros = lambda n: jnp.zeros(lead + (n,), ref.dtype)
    n_lo = HALF_LANES - HALF_ROPE
    parts = [
        rope[..., :HALF_ROPE] if rope is not None else zeros(HALF_ROPE),
        nope[..., :n_lo] if nope is not None else zeros(n_lo),
        rope[..., HALF_ROPE:] if rope is not None else zeros(HALF_ROPE),
        nope[..., n_lo:] if nope is not None else zeros(NOPE_DIM - n_lo),
        zeros(LANES - NOPE_DIM - ROPE_DIM),
    ]
    return jnp.concatenate(parts, axis=-1)


def _prep_w_in(w_in):
    o2 = Q_RANK + KV_RANK
    o3 = o2 + ROPE_DIM
    kr = _head_lanes(None, w_in[:, o2:o3])
    return jnp.concatenate([w_in[:, :o2], kr, w_in[:, o3:]], axis=1).astype(jnp.bfloat16)


def _prep_w_uq(w_uq):
    w = w_uq.reshape(Q_RANK, N_HEADS, NOPE_DIM + ROPE_DIM)
    wq = _head_lanes(w[..., :NOPE_DIM], w[..., NOPE_DIM:])
    return wq.reshape(Q_RANK, QK_WIDTH).astype(jnp.bfloat16)


def _prep_w_ukv(w_ukv):
    w = w_ukv.reshape(KV_RANK, N_HEADS, NOPE_DIM + V_DIM)
    k_nope, v = w[..., :NOPE_DIM], w[..., NOPE_DIM:]
    wk = _head_lanes(k_nope, None).reshape(KV_RANK, QK_WIDTH)
    v_pad = jnp.concatenate([v, jnp.zeros((KV_RANK, N_HEADS, SUM_ROWS), w.dtype)], axis=-1)
    wv_t = v_pad.reshape(KV_RANK, VT_WIDTH).T
    return wk.astype(jnp.bfloat16), wv_t.astype(jnp.bfloat16)


def _ones_rows_column():
    col = np.zeros((N_HEADS, V_ROWS, 1), np.float32)
    col[:, V_DIM:, :] = 1.0
    return jnp.asarray(col.reshape(VT_WIDTH, 1))


def kernel(x, p, positions, mix_pre_norm, w_in, q_norm, kv_norm, w_uq, w_ukv, conv_w, attn_group_norm, conv_group_norm, w_out, mix_post_norm, ffn_pre_norm, w_gate, w_up, w_down, ffn_post_norm, w_ple_proj, ple_norm, w_ple_gate):
    B, S, D = x.shape
    depth = w_in.shape[0]
    bf16 = jnp.bfloat16
    row = lambda g: g.reshape(1, -1)
    invf = _inv_freq_column()
    vone = _ones_rows_column()
    posf = positions.astype(jnp.float32)[:, None, :]
    for i in range(depth):
        wk, wv = _prep_w_ukv(w_ukv[i])
        q, k, v, c = _proj_call(
            x, posf, row(mix_pre_norm[i]), _prep_w_in(w_in[i]), row(q_norm[i]),
            row(kv_norm[i]), _prep_w_uq(w_uq[i]), wk, wv, vone, conv_w[i],
            row(conv_group_norm[i]), invf)
        a = _attn_call(q, k, v)
        out = _post_call(
            x.reshape(B * S, D), a,
            c.reshape(B * S, CONV_WIDTH), p[i].reshape(B * S, PLE_DIM),
            row(attn_group_norm[i]), w_out[i].astype(bf16), row(mix_post_norm[i]),
            row(ffn_pre_norm[i]), w_gate[i].astype(bf16), w_up[i].astype(bf16),
            w_down[i].astype(bf16), row(ffn_post_norm[i]),
            w_ple_proj[i].astype(bf16), row(ple_norm[i]), w_ple_gate[i].astype(bf16))
        x = out.reshape(B, S, D)
    return x
```

```python
import numpy as np
import jax
import jax.numpy as jnp
from jax import lax
from jax.experimental import pallas as pl
from jax.experimental.pallas import tpu as pltpu

D_MODEL = 1024
PLE_DIM = 256
ATTN_WIDTH = 512
CONV_WIDTH = 512
V_DIM = 64
NOPE_DIM = 64
ROPE_DIM = 32
HALF_ROPE = ROPE_DIM // 2
N_HEADS = 8
Q_RANK = 256
KV_RANK = 256
CONV_K = 3
D_FF = 2816
ROPE_THETA = 10000.0
EPS = 1e-6
LOG2_E = 1.4426950408889634

LANES = 128
HALF_LANES = LANES // 2
QK_WIDTH = N_HEADS * LANES
HEADS_PER_STEP = 2
N_PAIRS = N_HEADS // HEADS_PER_STEP
PAIR_LANES = HEADS_PER_STEP * LANES
SUM_ROWS = 16
V_ROWS = V_DIM + SUM_ROWS
VT_WIDTH = N_HEADS * V_ROWS
FF_CHUNK = 256

PROJ_TM = 1024
PROJ_ROW_GROUPS = 2
ATTN_TQ = 256
QK_STREAMS = 4
POST_TM = 1024
POST_ROW_GROUPS = 2
VMEM_LIMIT = 56 * 1024 * 1024
POST_VMEM_LIMIT = 60 * 1024 * 1024

_O_CQ = 0
_O_CKV = _O_CQ + Q_RANK
_O_KR = _O_CKV + KV_RANK
_O_BG = _O_KR + LANES
_O_CG = _O_BG + CONV_WIDTH
_O_XV = _O_CG + CONV_WIDTH
IN_EXT = _O_XV + CONV_WIDTH


_DONE = object()


def _rms(x, g):
    ms = jnp.mean(x * x, axis=-1, keepdims=True)
    return x * lax.rsqrt(ms + EPS) * g


def _dot(a, b):
    return jnp.dot(a, b, preferred_element_type=jnp.float32)


def _dependent_zero(x):
    bits = lax.bitcast_convert_type(x, jnp.int32)
    bits = lax.shift_right_logical(lax.shift_right_logical(bits, 16), 16)
    return bits.astype(jnp.float32)


def _dot_nt(a, b):
    return lax.dot_general(a, b, (((1,), (1,)), ((), ())),
                           preferred_element_type=jnp.float32)


def _proj_kernel(x_ref, pos_ref, gpre_ref, win_ref, qn_ref, kvn_ref, wq_ref,
                 wk_ref, wv_ref, vone_ref, cw_ref, cn_ref, invf_ref,
                 q_ref, k_ref, v_ref, c_ref, utail_ref):
    tm = x_ref.shape[1]
    bf16 = jnp.bfloat16
    rg = tm // PROJ_ROW_GROUPS

    @pl.when(pl.program_id(1) == 0)
    def _():
        utail_ref[...] = jnp.zeros_like(utail_ref)

    def rows_pipeline(rows):
        h = _rms(x_ref[0, rows, :], gpre_ref[...]).astype(bf16)
        yield
        z = _dot(h, win_ref[...])
        yield
        cq = _rms(z[:, _O_CQ:_O_CQ + Q_RANK], qn_ref[...]).astype(bf16)
        qq = _dot(cq, wq_ref[...])
        ckv = _rms(z[:, _O_CKV:_O_CKV + KV_RANK], kvn_ref[...]).astype(bf16)
        kk = _dot(ckv, wk_ref[...])
        v_ref[0, :, rows] = (_dot_nt(wv_ref[...], ckv) + vone_ref[...]).astype(bf16)

        ang_t = invf_ref[...] * pos_ref[0, :, rows]
        cos_t = jnp.cos(ang_t)
        sin_t = jnp.sin(ang_t)
        one_t = jnp.ones((HALF_LANES - HALF_ROPE, rg), jnp.float32)
        zero_t = jnp.zeros((HALF_LANES - HALF_ROPE, rg), jnp.float32)
        cos = jnp.concatenate([cos_t, one_t, cos_t, one_t], axis=0).T
        sin = jnp.concatenate([-sin_t, zero_t, sin_t, zero_t], axis=0).T
        scale = (NOPE_DIM + ROPE_DIM) ** -0.5 * LOG2_E
        cos_q = cos * scale
        sin_q = sin * scale
        kr = z[:, _O_KR:_O_KR + LANES]
        kr = kr * cos + pltpu.roll(kr, HALF_LANES, axis=1) * sin

        u = z[:, _O_CG:_O_CG + CONV_WIDTH] * z[:, _O_XV:_O_XV + CONV_WIDTH]
        tail = utail_ref[...]
        row = lax.broadcasted_iota(jnp.int32, (rg, CONV_WIDTH), 0)
        u1 = pltpu.roll(u, 1, axis=0)
        u1 = jnp.where(row == 0, tail[7:8, :], u1)
        u2 = pltpu.roll(u, 2, axis=0)
        u2 = jnp.where(row == 0, tail[6:7, :], jnp.where(row == 1, tail[7:8, :], u2))
        utail_ref[...] = u[rg - 8:rg, :]
        cw = cw_ref[...]
        y = cw[0:1, :] * u + cw[1:2, :] * u1 + cw[2:3, :] * u2
        cc = z[:, _O_BG:_O_BG + CONV_WIDTH] * y
        c_ref[0, rows, :] = _rms(cc, cn_ref[...]).astype(bf16)
        yield

        for hh in range(N_HEADS):
            lo = hh * LANES
            qh = qq[:, lo:lo + LANES]
            qh = qh * cos_q + pltpu.roll(qh, HALF_LANES, axis=1) * sin_q
            pair, ps = divmod(hh, HEADS_PER_STEP)
            pl_ = slice(ps * LANES, (ps + 1) * LANES)
            q_ref[0, pair, rows, pl_] = qh.astype(bf16)
            k_ref[0, pair, rows, pl_] = (kk[:, lo:lo + LANES] + kr).astype(bf16)

    live = []
    for r in range(PROJ_ROW_GROUPS):
        live.insert(0, rows_pipeline(slice(r * rg, (r + 1) * rg)))
        for _ in range(2):
            live = [g for g in live if next(g, _DONE) is not _DONE]
    while live:
        live = [g for g in live if next(g, _DONE) is not _DONE]


def _const_spec(shape):
    nd = len(shape)
    return pl.BlockSpec(shape, lambda *_: (0,) * nd, pipeline_mode=pl.Buffered(1))


def _proj_call(x, posf, gpre, win, qn, kvn, wq, wk, wv, vone, cw, cn, invf):
    B, S, _ = x.shape
    tm = PROJ_TM
    grid = (B, S // tm)
    tok = lambda w: pl.BlockSpec((1, tm, w), lambda b, s: (b, s, 0))
    pos_spec = pl.BlockSpec((1, 1, tm), lambda b, s: (b, 0, s))
    qk_spec = pl.BlockSpec((1, N_PAIRS, tm, PAIR_LANES), lambda b, s: (b, 0, s, 0))
    consts = (gpre, win, qn, kvn, wq, wk, wv, vone, cw, cn, invf)
    bf16 = jnp.bfloat16
    return pl.pallas_call(
        _proj_kernel,
        grid=grid,
        in_specs=[tok(D_MODEL), pos_spec] + [_const_spec(c.shape) for c in consts],
        out_specs=[qk_spec, qk_spec,
                   pl.BlockSpec((1, VT_WIDTH, tm), lambda b, s: (b, 0, s)),
                   tok(CONV_WIDTH)],
        out_shape=[jax.ShapeDtypeStruct((B, N_PAIRS, S, PAIR_LANES), bf16),
                   jax.ShapeDtypeStruct((B, N_PAIRS, S, PAIR_LANES), bf16),
                   jax.ShapeDtypeStruct((B, VT_WIDTH, S), bf16),
                   jax.ShapeDtypeStruct((B, S, CONV_WIDTH), bf16)],
        scratch_shapes=[pltpu.VMEM((8, CONV_WIDTH), jnp.float32)],
        compiler_params=pltpu.CompilerParams(
            dimension_semantics=("arbitrary", "arbitrary"),
            vmem_limit_bytes=VMEM_LIMIT),
        name="proj",
    )(x, posf, *consts)


def _attn_kernel(q_ref, k_ref, vt_ref, o_ref, s_scr, p_scr):
    S = q_ref.shape[2]
    tq = ATTN_TQ
    bf16 = jnp.bfloat16
    neg = jnp.finfo(jnp.float32).min
    key = lax.broadcasted_iota(jnp.int32, (tq, tq), 0)
    qry = lax.broadcasted_iota(jnp.int32, (tq, tq), 1)
    causal = key <= qry
    nt = S // tq
    tiles = list(range(0, nt, 2)) + list(range(nt - 1 - nt % 2, 0, -2))
    bodies = [(i, hh) for i in tiles for hh in range(HEADS_PER_STEP)]

    def chunk(c):
        return slice(c * tq, (c + 1) * tq)

    def one_pair(pair, carry):
        col_max = {}
        head_out = {}
        pace = {"tok": jnp.zeros((8, tq), jnp.float32)}

        def scores(n):
            i, hh = bodies[n]
            hs = slice(hh * LANES, (hh + 1) * LANES)
            q = q_ref[0, pair, chunk(i), hs]
            part = None
            per = -(-(i + 1) // QK_STREAMS)
            starts = list(range(0, i + 1, per))
            streams = [_dot_nt(k_ref[0, pair, c0 * tq:min(c0 + per, i + 1) * tq, hs], q)
                       for c0 in starts]
            order = [c0 + j for j in range(per) for c0 in starts
                     if c0 + j <= min(c0 + per - 1, i)]
            for pos, c in enumerate(order):
                s = streams[c // per][chunk(c % per), :]
                if c == i:
                    s = jnp.where(causal, s, neg)
                s_scr[n % 2, chunk(c), :] = s
                cm = jnp.max(s.reshape(tq // 8, 8, tq), axis=0)
                part = cm if part is None else jnp.maximum(part, cm)
                pace["tok"] = cm
                if pos == len(order) - 1:
                    col_max[n] = jnp.max(part, axis=0, keepdims=True)
                yield

        def exps(n):
            i, _ = bodies[n]
            m = col_max.pop(n)
            for c in range(i + 1):
                m_c = m + _dependent_zero(pace["tok"])[0:1, :]
                p_scr[n % 2, chunk(c), :] = jnp.exp2(
                    (s_scr[n % 2, chunk(c), :] - m_c).astype(bf16))
                yield

        def values(n):
            i, hh = bodies[n]
            acc = None
            for c in range(i + 1):
                vt = vt_ref[0, pair, hh * V_ROWS:(hh + 1) * V_ROWS, chunk(c)]
                d = _dot(vt, p_scr[n % 2, chunk(c), :])
                acc = d if acc is None else acc + d
                if c == i:
                    head_out[hh] = acc[0:V_DIM, :] * (1.0 / acc[V_DIM:V_DIM + 1, :])
                    if hh == HEADS_PER_STEP - 1:
                        o_ref[0, pair, chunk(i), :] = jnp.concatenate(
                            [head_out.pop(h) for h in range(HEADS_PER_STEP)], axis=0).T
                yield

        for step in range(len(bodies) + 2):
            live = []
            if step < len(bodies):
                live.append(scores(step))
            if 0 <= step - 1 < len(bodies):
                live.append(exps(step - 1))
            if 0 <= step - 2 < len(bodies):
                live.append(values(step - 2))
            while live:
                live = [g for g in live if next(g, _DONE) is not _DONE]
        return carry

    lax.fori_loop(0, q_ref.shape[1], one_pair, 0)


def _attn_call(q, k, vt):
    B, _, S, _ = q.shape
    vt = vt.reshape(B, N_PAIRS, HEADS_PER_STEP * V_ROWS, S)
    grid = (B,)
    qk_spec = pl.BlockSpec((1, N_PAIRS, S, PAIR_LANES), lambda b: (b, 0, 0, 0))
    vt_spec = pl.BlockSpec((1, N_PAIRS, HEADS_PER_STEP * V_ROWS, S), lambda b: (b, 0, 0, 0))
    o_spec = pl.BlockSpec((1, N_PAIRS, S, HEADS_PER_STEP * V_DIM), lambda b: (b, 0, 0, 0))
    return pl.pallas_call(
        _attn_kernel,
        grid=grid,
        in_specs=[qk_spec, qk_spec, vt_spec],
        out_specs=o_spec,
        out_shape=jax.ShapeDtypeStruct((B, N_PAIRS, S, HEADS_PER_STEP * V_DIM), jnp.float32),
        scratch_shapes=[pltpu.VMEM((2, S, ATTN_TQ), jnp.float32),
                        pltpu.VMEM((2, S, ATTN_TQ), jnp.bfloat16)],
        compiler_params=pltpu.CompilerParams(
            dimension_semantics=("arbitrary",),
            vmem_limit_bytes=VMEM_LIMIT),
        name="attn",
    )(q, k, vt)


def _post_kernel(x_ref, a_ref, c_ref, p_ref, an_ref, wo_ref, mpn_ref, fpre_ref,
                 wg_ref, wu_ref, wd_ref, fpost_ref, wpp_ref, pn_ref, wpg_ref,
                 o_ref):
    bf16 = jnp.bfloat16
    rg = x_ref.shape[0] // POST_ROW_GROUPS

    def rows_pipeline(rows):
        a = jnp.concatenate([a_ref[0, j, rows, :] for j in range(N_PAIRS)], axis=-1)
        a = _rms(a, an_ref[...]).astype(bf16)
        m = jnp.concatenate([a, c_ref[rows, :]], axis=-1)
        y = _dot(m, wo_ref[...])
        yield
        x1 = x_ref[rows, :] + _rms(y, mpn_ref[...])
        h = _rms(x1, fpre_ref[...]).astype(bf16)
        yield
        f = None
        for j in range(D_FF // FF_CHUNK):
            cs = slice(j * FF_CHUNK, (j + 1) * FF_CHUNK)
            g = _dot(h, wg_ref[:, cs])
            u = _dot(h, wu_ref[:, cs])
            act = (g * jax.nn.sigmoid(g) * u).astype(bf16)
            part = _dot(act, wd_ref[cs, :])
            f = part if f is None else f + part
            yield
        x2 = x1 + _rms(f, fpost_ref[...])
        e = _rms(_dot(p_ref[rows, :].astype(bf16), wpp_ref[...]), pn_ref[...])
        gate = jax.nn.sigmoid(_dot(x2.astype(bf16), wpg_ref[...]))
        o_ref[rows, :] = x2 + gate * e

    live = []
    for r in range(POST_ROW_GROUPS):
        live.insert(0, rows_pipeline(slice(r * rg, (r + 1) * rg)))
        for _ in range(2):
            live = [g for g in live if next(g, _DONE) is not _DONE]
    while live:
        live = [g for g in live if next(g, _DONE) is not _DONE]


def _post_call(x2d, a4d, c2d, p2d, an, wo, mpn, fpre, wg, wu, wd, fpost, wpp, pn, wpg):
    T = x2d.shape[0]
    tm = POST_TM
    tiles_per_row = a4d.shape[2] // tm
    tok = lambda w: pl.BlockSpec((tm, w), lambda t: (t, 0))
    a_spec = pl.BlockSpec((1, N_PAIRS, tm, a4d.shape[3]),
                          lambda t: (t // tiles_per_row, 0, t % tiles_per_row, 0))
    consts = (an, wo, mpn, fpre, wg, wu, wd, fpost, wpp, pn, wpg)
    return pl.pallas_call(
        _post_kernel,
        grid=(T // tm,),
        in_specs=[tok(D_MODEL), a_spec, tok(CONV_WIDTH), tok(PLE_DIM)]
        + [_const_spec(c.shape) for c in consts],
        out_specs=tok(D_MODEL),
        out_shape=jax.ShapeDtypeStruct((T, D_MODEL), jnp.float32),
        compiler_params=pltpu.CompilerParams(
            dimension_semantics=("arbitrary",),
            vmem_limit_bytes=POST_VMEM_LIMIT),
        name="post",
    )(x2d, a4d, c2d, p2d, *consts)


def _inv_freq_column():
    inv_freq = 1.0 / (ROPE_THETA ** (np.arange(HALF_ROPE, dtype=np.float32) / HALF_ROPE))
    return jnp.asarray(inv_freq.astype(np.float32).reshape(HALF_ROPE, 1))


def _head_lanes(nope, rope):
    ref = nope if nope is not None else rope
    lead = ref.shape[:-1]
    zeros = lambda n: jnp.zeros(lead + (n,), ref.dtype)
    n_lo = HALF_LANES - HALF_ROPE
    parts = [
        rope[..., :HALF_ROPE] if rope is not None else zeros(HALF_ROPE),
        nope[..., :n_lo] if nope is not None else zeros(n_lo),
        rope[..., HALF_ROPE:] if rope is not None else zeros(HALF_ROPE),
        nope[..., n_lo:] if nope is not None else zeros(NOPE_DIM - n_lo),
        zeros(LANES - NOPE_DIM - ROPE_DIM),
    ]
    return jnp.concatenate(parts, axis=-1)


def _prep_w_in(w_in):
    o2 = Q_RANK + KV_RANK
    o3 = o2 + ROPE_DIM
    kr = _head_lanes(None, w_in[:, o2:o3])
    return jnp.concatenate([w_in[:, :o2], kr, w_in[:, o3:]], axis=1).astype(jnp.bfloat16)


def _prep_w_uq(w_uq):
    w = w_uq.reshape(Q_RANK, N_HEADS, NOPE_DIM + ROPE_DIM)
    wq = _head_lanes(w[..., :NOPE_DIM], w[..., NOPE_DIM:])
    return wq.reshape(Q_RANK, QK_WIDTH).astype(jnp.bfloat16)


def _prep_w_ukv(w_ukv):
    w = w_ukv.reshape(KV_RANK, N_HEADS, NOPE_DIM + V_DIM)
    k_nope, v = w[..., :NOPE_DIM], w[..., NOPE_DIM:]
    wk = _head_lanes(k_nope, None).reshape(KV_RANK, QK_WIDTH)
    v_pad = jnp.concatenate([v, jnp.zeros((KV_RANK, N_HEADS, SUM_ROWS), w.dtype)], axis=-1)
    wv_t = v_pad.reshape(KV_RANK, VT_WIDTH).T
    return wk.astype(jnp.bfloat16), wv_t.astype(jnp.bfloat16)


def _ones_rows_column():
    col = np.zeros((N_HEADS, V_ROWS, 1), np.float32)
    col[:, V_DIM:, :] = 1.0
    return jnp.asarray(col.reshape(VT_WIDTH, 1))


def kernel(x, p, positions, mix_pre_norm, w_in, q_norm, kv_norm, w_uq, w_ukv, conv_w, attn_group_norm, conv_group_norm, w_out, mix_post_norm, ffn_pre_norm, w_gate, w_up, w_down, ffn_post_norm, w_ple_proj, ple_norm, w_ple_gate):
    B, S, D = x.shape
    depth = w_in.shape[0]
    bf16 = jnp.bfloat16
    row = lambda g: g.reshape(1, -1)
    invf = _inv_freq_column()
    vone = _ones_rows_column()
    posf = positions.astype(jnp.float32)[:, None, :]
    for i in range(depth):
        wk, wv = _prep_w_ukv(w_ukv[i])
        q, k, v, c = _proj_call(
            x, posf, row(mix_pre_norm[i]), _prep_w_in(w_in[i]), row(q_norm[i]),
            row(kv_norm[i]), _prep_w_uq(w_uq[i]), wk, wv, vone, conv_w[i],
            row(conv_group_norm[i]), invf)
        a = _attn_call(q, k, v)
        out = _post_call(
            x.reshape(B * S, D), a,
            c.reshape(B * S, CONV_WIDTH), p[i].reshape(B * S, PLE_DIM),
            row(attn_group_norm[i]), w_out[i].astype(bf16), row(mix_post_norm[i]),
            row(ffn_pre_norm[i]), w_gate[i].astype(bf16), w_up[i].astype(bf16),
            w_down[i].astype(bf16), row(ffn_post_norm[i]),
            w_ple_proj[i].astype(bf16), row(ple_norm[i]), w_ple_gate[i].astype(bf16))
        x = out.reshape(B, S, D)
    return x
```

```python
import numpy as np
import jax
import jax.numpy as jnp
from jax import lax
from jax.experimental import pallas as pl
from jax.experimental.pallas import tpu as pltpu

D_MODEL = 1024
PLE_DIM = 256
ATTN_WIDTH = 512
CONV_WIDTH = 512
V_DIM = 64
NOPE_DIM = 64
ROPE_DIM = 32
HALF_ROPE = ROPE_DIM // 2
N_HEADS = 8
Q_RANK = 256
KV_RANK = 256
CONV_K = 3
D_FF = 2816
ROPE_THETA = 10000.0
EPS = 1e-6
LOG2_E = 1.4426950408889634

LANES = 128
HALF_LANES = LANES // 2
QK_WIDTH = N_HEADS * LANES
HEADS_PER_STEP = 2
N_PAIRS = N_HEADS // HEADS_PER_STEP
PAIR_LANES = HEADS_PER_STEP * LANES
SUM_ROWS = 16
V_ROWS = V_DIM + SUM_ROWS
VT_WIDTH = N_HEADS * V_ROWS
FF_CHUNK = 256

PROJ_TM = 1024
PROJ_ROW_GROUPS = 2
ATTN_TQ = 256
QK_STREAMS = 4
POST_TM = 1024
POST_ROW_GROUPS = 2
VMEM_LIMIT = 56 * 1024 * 1024
POST_VMEM_LIMIT = 60 * 1024 * 1024

_DONE = object()


def _rms(x, g):
    ms = jnp.mean(x * x, axis=-1, keepdims=True)
    return x * lax.rsqrt(ms + EPS) * g


def _dot(a, b):
    return jnp.dot(a, b, preferred_element_type=jnp.float32)


def _dependent_zero(x):
    bits = lax.bitcast_convert_type(x, jnp.int32)
    bits = lax.shift_right_logical(lax.shift_right_logical(bits, 16), 16)
    return bits.astype(jnp.float32)


def _dot_nt(a, b):
    return lax.dot_general(a, b, (((1,), (1,)), ((), ())),
                           preferred_element_type=jnp.float32)


def _proj_kernel(x_ref, pos_ref, gpre_ref, wlat_ref, wkr_ref, wconv_ref, qn_ref, kvn_ref, wq_ref,
                 wk_ref, wv_ref, vone_ref, cw_ref, cn_ref, invf_ref,
                 q_ref, k_ref, v_ref, c_ref, utail_ref):
    tm = x_ref.shape[1]
    bf16 = jnp.bfloat16
    rg = tm // PROJ_ROW_GROUPS

    @pl.when(pl.program_id(1) == 0)
    def _():
        utail_ref[...] = jnp.zeros_like(utail_ref)

    def rows_pipeline(rows):
        h = _rms(x_ref[0, rows, :], gpre_ref[...]).astype(bf16)
        yield
        z_lat = _dot(h, wlat_ref[...])
        z_kr = _dot(h, wkr_ref[...])
        z_conv = _dot(h, wconv_ref[...])
        yield
        cq = _rms(z_lat[:, :Q_RANK], qn_ref[...]).astype(bf16)
        qq = _dot(cq, wq_ref[...])
        ckv = _rms(z_lat[:, Q_RANK:], kvn_ref[...]).astype(bf16)
        kk = _dot(ckv, wk_ref[...])
        v_ref[0, :, rows] = (_dot_nt(wv_ref[...], ckv) + vone_ref[...]).astype(bf16)

        ang_t = invf_ref[...] * pos_ref[0, :, rows]
        cos_t = jnp.cos(ang_t)
        sin_t = jnp.sin(ang_t)
        one_t = jnp.ones((HALF_LANES - HALF_ROPE, rg), jnp.float32)
        zero_t = jnp.zeros((HALF_LANES - HALF_ROPE, rg), jnp.float32)
        cos = jnp.concatenate([cos_t, one_t, cos_t, one_t], axis=0).T
        sin = jnp.concatenate([-sin_t, zero_t, sin_t, zero_t], axis=0).T
        scale = (NOPE_DIM + ROPE_DIM) ** -0.5 * LOG2_E
        cos_q = cos * scale
        sin_q = sin * scale
        kr = z_kr * cos + pltpu.roll(z_kr, HALF_LANES, axis=1) * sin

        bg = z_conv[:, :CONV_WIDTH]
        u = z_conv[:, CONV_WIDTH:2 * CONV_WIDTH] * z_conv[:, 2 * CONV_WIDTH:]
        tail = utail_ref[...]
        row = lax.broadcasted_iota(jnp.int32, (rg, CONV_WIDTH), 0)
        u1 = pltpu.roll(u, 1, axis=0)
        u1 = jnp.where(row == 0, tail[7:8, :], u1)
        u2 = pltpu.roll(u, 2, axis=0)
        u2 = jnp.where(row == 0, tail[6:7, :], jnp.where(row == 1, tail[7:8, :], u2))
        utail_ref[...] = u[rg - 8:rg, :]
        cw = cw_ref[...]
        y = cw[0:1, :] * u + cw[1:2, :] * u1 + cw[2:3, :] * u2
        cc = bg * y
        c_ref[0, rows, :] = _rms(cc, cn_ref[...]).astype(bf16)
        yield

        for hh in range(N_HEADS):
            lo = hh * LANES
            qh = qq[:, lo:lo + LANES]
            qh = qh * cos_q + pltpu.roll(qh, HALF_LANES, axis=1) * sin_q
            pair, ps = divmod(hh, HEADS_PER_STEP)
            pl_ = slice(ps * LANES, (ps + 1) * LANES)
            q_ref[0, pair, rows, pl_] = qh.astype(bf16)
            k_ref[0, pair, rows, pl_] = (kk[:, lo:lo + LANES] + kr).astype(bf16)

    live = []
    for r in range(PROJ_ROW_GROUPS):
        live.insert(0, rows_pipeline(slice(r * rg, (r + 1) * rg)))
        for _ in range(2):
            live = [g for g in live if next(g, _DONE) is not _DONE]
    while live:
        live = [g for g in live if next(g, _DONE) is not _DONE]


def _const_spec(shape):
    nd = len(shape)
    return pl.BlockSpec(shape, lambda *_: (0,) * nd, pipeline_mode=pl.Buffered(1))


def _proj_call(x, posf, gpre, wlat, wkr, wconv, qn, kvn, wq, wk, wv, vone, cw, cn, invf):
    B, S, _ = x.shape
    tm = PROJ_TM
    grid = (B, S // tm)
    tok = lambda w: pl.BlockSpec((1, tm, w), lambda b, s: (b, s, 0))
    pos_spec = pl.BlockSpec((1, 1, tm), lambda b, s: (b, 0, s))
    qk_spec = pl.BlockSpec((1, N_PAIRS, tm, PAIR_LANES), lambda b, s: (b, 0, s, 0))
    consts = (gpre, wlat, wkr, wconv, qn, kvn, wq, wk, wv, vone, cw, cn, invf)
    bf16 = jnp.bfloat16
    return pl.pallas_call(
        _proj_kernel,
        grid=grid,
        in_specs=[tok(D_MODEL), pos_spec] + [_const_spec(c.shape) for c in consts],
        out_specs=[qk_spec, qk_spec,
                   pl.BlockSpec((1, VT_WIDTH, tm), lambda b, s: (b, 0, s)),
                   tok(CONV_WIDTH)],
        out_shape=[jax.ShapeDtypeStruct((B, N_PAIRS, S, PAIR_LANES), bf16),
                   jax.ShapeDtypeStruct((B, N_PAIRS, S, PAIR_LANES), bf16),
                   jax.ShapeDtypeStruct((B, VT_WIDTH, S), bf16),
                   jax.ShapeDtypeStruct((B, S, CONV_WIDTH), bf16)],
        scratch_shapes=[pltpu.VMEM((8, CONV_WIDTH), jnp.float32)],
        compiler_params=pltpu.CompilerParams(
            dimension_semantics=("arbitrary", "arbitrary"),
            vmem_limit_bytes=VMEM_LIMIT),
        name="proj",
    )(x, posf, *consts)


def _attn_kernel(q_ref, k_ref, vt_ref, o_ref, s_scr, p_scr):
    S = q_ref.shape[2]
    tq = ATTN_TQ
    bf16 = jnp.bfloat16
    neg = jnp.finfo(jnp.float32).min
    key = lax.broadcasted_iota(jnp.int32, (tq, tq), 0)
    qry = lax.broadcasted_iota(jnp.int32, (tq, tq), 1)
    causal = key <= qry
    nt = S // tq
    tiles = list(range(0, nt, 2)) + list(range(nt - 1 - nt % 2, 0, -2))
    bodies = [(i, hh) for i in tiles for hh in range(HEADS_PER_STEP)]

    def chunk(c):
        return slice(c * tq, (c + 1) * tq)

    def one_pair(pair, carry):
        col_max = {}
        head_out = {}
        pace = {"tok": jnp.zeros((8, tq), jnp.float32)}

        def scores(n):
            i, hh = bodies[n]
            hs = slice(hh * LANES, (hh + 1) * LANES)
            q = q_ref[0, pair, chunk(i), hs]
            part = None
            per = -(-(i + 1) // QK_STREAMS)
            starts = list(range(0, i + 1, per))
            streams = [_dot_nt(k_ref[0, pair, c0 * tq:min(c0 + per, i + 1) * tq, hs], q)
                       for c0 in starts]
            order = [c0 + j for j in range(per) for c0 in starts
                     if c0 + j <= min(c0 + per - 1, i)]
            for pos, c in enumerate(order):
                s = streams[c // per][chunk(c % per), :]
                if c == i:
                    s = jnp.where(causal, s, neg)
                s_scr[n % 2, chunk(c), :] = s
                cm = jnp.max(s.reshape(tq // 8, 8, tq), axis=0)
                part = cm if part is None else jnp.maximum(part, cm)
                pace["tok"] = cm
                if pos == len(order) - 1:
                    col_max[n] = jnp.max(part, axis=0, keepdims=True)
                yield

        def exps(n):
            i, _ = bodies[n]
            m = col_max.pop(n)
            for c in range(i + 1):
                m_c = m + _dependent_zero(pace["tok"])[0:1, :]
                p_scr[n % 2, chunk(c), :] = jnp.exp2(
                    (s_scr[n % 2, chunk(c), :] - m_c).astype(bf16))
                yield

        def values(n):
            i, hh = bodies[n]
            acc = None
            for c in range(i + 1):
                vt = vt_ref[0, pair, hh * V_ROWS:(hh + 1) * V_ROWS, chunk(c)]
                d = _dot(vt, p_scr[n % 2, chunk(c), :])
                acc = d if acc is None else acc + d
                if c == i:
                    head_out[hh] = acc[0:V_DIM, :] * (1.0 / acc[V_DIM:V_DIM + 1, :])
                    if hh == HEADS_PER_STEP - 1:
                        o_ref[0, pair, chunk(i), :] = jnp.concatenate(
                            [head_out.pop(h) for h in range(HEADS_PER_STEP)], axis=0).T
                yield

        for step in range(len(bodies) + 2):
            live = []
            if step < len(bodies):
                live.append(scores(step))
            if 0 <= step - 1 < len(bodies):
                live.append(exps(step - 1))
            if 0 <= step - 2 < len(bodies):
                live.append(values(step - 2))
            while live:
                live = [g for g in live if next(g, _DONE) is not _DONE]
        return carry

    lax.fori_loop(0, q_ref.shape[1], one_pair, 0)


def _attn_call(q, k, vt):
    B, _, S, _ = q.shape
    vt = vt.reshape(B, N_PAIRS, HEADS_PER_STEP * V_ROWS, S)
    grid = (B,)
    qk_spec = pl.BlockSpec((1, N_PAIRS, S, PAIR_LANES), lambda b: (b, 0, 0, 0))
    vt_spec = pl.BlockSpec((1, N_PAIRS, HEADS_PER_STEP * V_ROWS, S), lambda b: (b, 0, 0, 0))
    o_spec = pl.BlockSpec((1, N_PAIRS, S, HEADS_PER_STEP * V_DIM), lambda b: (b, 0, 0, 0))
    return pl.pallas_call(
        _attn_kernel,
        grid=grid,
        in_specs=[qk_spec, qk_spec, vt_spec],
        out_specs=o_spec,
        out_shape=jax.ShapeDtypeStruct((B, N_PAIRS, S, HEADS_PER_STEP * V_DIM), jnp.float32),
        scratch_shapes=[pltpu.VMEM((2, S, ATTN_TQ), jnp.float32),
                        pltpu.VMEM((2, S, ATTN_TQ), jnp.bfloat16)],
        compiler_params=pltpu.CompilerParams(
            dimension_semantics=("arbitrary",),
            vmem_limit_bytes=VMEM_LIMIT),
        name="attn",
    )(q, k, vt)


def _post_kernel(x_ref, a_ref, c_ref, p_ref, an_ref, wo_ref, mpn_ref, fpre_ref,
                 wg_ref, wu_ref, wd_ref, fpost_ref, wpp_ref, pn_ref, wpg_ref,
                 o_ref):
    bf16 = jnp.bfloat16
    rg = x_ref.shape[0] // POST_ROW_GROUPS

    def rows_pipeline(rows):
        a = jnp.concatenate([a_ref[0, j, rows, :] for j in range(N_PAIRS)], axis=-1)
        a = _rms(a, an_ref[...]).astype(bf16)
        m = jnp.concatenate([a, c_ref[rows, :]], axis=-1)
        y = _dot(m, wo_ref[...])
        yield
        x1 = x_ref[rows, :] + _rms(y, mpn_ref[...])
        h = _rms(x1, fpre_ref[...]).astype(bf16)
        yield
        f = None
        for j in range(D_FF // FF_CHUNK):
            cs = slice(j * FF_CHUNK, (j + 1) * FF_CHUNK)
            g = _dot(h, wg_ref[:, cs])
            u = _dot(h, wu_ref[:, cs])
            act = (g * jax.nn.sigmoid(g) * u).astype(bf16)
            part = _dot(act, wd_ref[cs, :])
            f = part if f is None else f + part
            yield
        x2 = x1 + _rms(f, fpost_ref[...])
        e = _rms(_dot(p_ref[rows, :].astype(bf16), wpp_ref[...]), pn_ref[...])
        gate = jax.nn.sigmoid(_dot(x2.astype(bf16), wpg_ref[...]))
        o_ref[rows, :] = x2 + gate * e

    live = []
    for r in range(POST_ROW_GROUPS):
        live.insert(0, rows_pipeline(slice(r * rg, (r + 1) * rg)))
        for _ in range(2):
            live = [g for g in live if next(g, _DONE) is not _DONE]
    while live:
        live = [g for g in live if next(g, _DONE) is not _DONE]


def _post_call(x2d, a4d, c2d, p2d, an, wo, mpn, fpre, wg, wu, wd, fpost, wpp, pn, wpg):
    T = x2d.shape[0]
    tm = POST_TM
    tiles_per_row = a4d.shape[2] // tm
    tok = lambda w: pl.BlockSpec((tm, w), lambda t: (t, 0))
    a_spec = pl.BlockSpec((1, N_PAIRS, tm, a4d.shape[3]),
                          lambda t: (t // tiles_per_row, 0, t % tiles_per_row, 0))
    consts = (an, wo, mpn, fpre, wg, wu, wd, fpost, wpp, pn, wpg)
    return pl.pallas_call(
        _post_kernel,
        grid=(T // tm,),
        in_specs=[tok(D_MODEL), a_spec, tok(CONV_WIDTH), tok(PLE_DIM)]
        + [_const_spec(c.shape) for c in consts],
        out_specs=tok(D_MODEL),
        out_shape=jax.ShapeDtypeStruct((T, D_MODEL), jnp.float32),
        compiler_params=pltpu.CompilerParams(
            dimension_semantics=("arbitrary",),
            vmem_limit_bytes=POST_VMEM_LIMIT),
        name="post",
    )(x2d, a4d, c2d, p2d, *consts)


def _inv_freq_column():
    inv_freq = 1.0 / (ROPE_THETA ** (np.arange(HALF_ROPE, dtype=np.float32) / HALF_ROPE))
    return jnp.asarray(inv_freq.astype(np.float32).reshape(HALF_ROPE, 1))


def _head_lanes(nope, rope):
    ref = nope if nope is not None else rope
    lead = ref.shape[:-1]
    zeros = lambda n: jnp.zeros(lead + (n,), ref.dtype)
    n_lo = HALF_LANES - HALF_ROPE
    parts = [
        rope[..., :HALF_ROPE] if rope is not None else zeros(HALF_ROPE),
        nope[..., :n_lo] if nope is not None else zeros(n_lo),
        rope[..., HALF_ROPE:] if rope is not None else zeros(HALF_ROPE),
        nope[..., n_lo:] if nope is not None else zeros(NOPE_DIM - n_lo),
        zeros(LANES - NOPE_DIM - ROPE_DIM),
    ]
    return jnp.concatenate(parts, axis=-1)


def _prep_w_in(w_in):
    o2 = Q_RANK + KV_RANK
    o3 = o2 + ROPE_DIM
    bf16 = jnp.bfloat16
    kr = _head_lanes(None, w_in[:, o2:o3])
    return w_in[:, :o2].astype(bf16), kr.astype(bf16), w_in[:, o3:].astype(bf16)


def _prep_w_uq(w_uq):
    w = w_uq.reshape(Q_RANK, N_HEADS, NOPE_DIM + ROPE_DIM)
    wq = _head_lanes(w[..., :NOPE_DIM], w[..., NOPE_DIM:])
    return wq.reshape(Q_RANK, QK_WIDTH).astype(jnp.bfloat16)


def _prep_w_ukv(w_ukv):
    w = w_ukv.reshape(KV_RANK, N_HEADS, NOPE_DIM + V_DIM)
    k_nope, v = w[..., :NOPE_DIM], w[..., NOPE_DIM:]
    wk = _head_lanes(k_nope, None).reshape(KV_RANK, QK_WIDTH)
    v_pad = jnp.concatenate([v, jnp.zeros((KV_RANK, N_HEADS, SUM_ROWS), w.dtype)], axis=-1)
    wv_t = v_pad.reshape(KV_RANK, VT_WIDTH).T
    return wk.astype(jnp.bfloat16), wv_t.astype(jnp.bfloat16)


def _ones_rows_column():
    col = np.zeros((N_HEADS, V_ROWS, 1), np.float32)
    col[:, V_DIM:, :] = 1.0
    return jnp.asarray(col.reshape(VT_WIDTH, 1))


def kernel(x, p, positions, mix_pre_norm, w_in, q_norm, kv_norm, w_uq, w_ukv, conv_w, attn_group_norm, conv_group_norm, w_out, mix_post_norm, ffn_pre_norm, w_gate, w_up, w_down, ffn_post_norm, w_ple_proj, ple_norm, w_ple_gate):
    B, S, D = x.shape
    depth = w_in.shape[0]
    bf16 = jnp.bfloat16
    row = lambda g: g.reshape(1, -1)
    invf = _inv_freq_column()
    vone = _ones_rows_column()
    posf = positions.astype(jnp.float32)[:, None, :]
    for i in range(depth):
        wk, wv = _prep_w_ukv(w_ukv[i])
        q, k, v, c = _proj_call(
            x, posf, row(mix_pre_norm[i]), *_prep_w_in(w_in[i]), row(q_norm[i]),
            row(kv_norm[i]), _prep_w_uq(w_uq[i]), wk, wv, vone, conv_w[i],
            row(conv_group_norm[i]), invf)
        a = _attn_call(q, k, v)
        out = _post_call(
            x.reshape(B * S, D), a,
            c.reshape(B * S, CONV_WIDTH), p[i].reshape(B * S, PLE_DIM),
            row(attn_group_norm[i]), w_out[i].astype(bf16), row(mix_post_norm[i]),
            row(ffn_pre_norm[i]), w_gate[i].astype(bf16), w_up[i].astype(bf16),
            w_down[i].astype(bf16), row(ffn_post_norm[i]),
            w_ple_proj[i].astype(bf16), row(ple_norm[i]), w_ple_gate[i].astype(bf16))
        x = out.reshape(B, S, D)
    return x
```

```python
import numpy as np
import jax
import jax.numpy as jnp
from jax import lax
from jax.experimental import pallas as pl
from jax.experimental.pallas import tpu as pltpu

D_MODEL = 1024
PLE_DIM = 256
ATTN_WIDTH = 512
CONV_WIDTH = 512
V_DIM = 64
NOPE_DIM = 64
ROPE_DIM = 32
HALF_ROPE = ROPE_DIM // 2
N_HEADS = 8
Q_RANK = 256
KV_RANK = 256
CONV_K = 3
D_FF = 2816
ROPE_THETA = 10000.0
EPS = 1e-6
LOG2_E = 1.4426950408889634

LANES = 128
HALF_LANES = LANES // 2
QK_WIDTH = N_HEADS * LANES
HEADS_PER_STEP = 2
N_PAIRS = N_HEADS // HEADS_PER_STEP
PAIR_LANES = HEADS_PER_STEP * LANES
SUM_ROWS = 16
V_ROWS = V_DIM + SUM_ROWS
VT_WIDTH = N_HEADS * V_ROWS
FF_CHUNK = 256

PROJ_TM = 1024
PROJ_ROW_GROUPS = 2
ATTN_TQ = 256
QK_STREAMS = 4
PAIRS_PER_TRIP = 2
POST_TM = 1024
POST_ROW_GROUPS = 2
VMEM_LIMIT = 56 * 1024 * 1024
POST_VMEM_LIMIT = 60 * 1024 * 1024

_DONE = object()


def _rms(x, g):
    ms = jnp.mean(x * x, axis=-1, keepdims=True)
    return x * lax.rsqrt(ms + EPS) * g


def _dot(a, b):
    return jnp.dot(a, b, preferred_element_type=jnp.float32)


def _dependent_zero(x):
    bits = lax.bitcast_convert_type(x, jnp.int32)
    bits = lax.shift_right_logical(lax.shift_right_logical(bits, 16), 16)
    return bits.astype(jnp.float32)


def _dot_nt(a, b):
    return lax.dot_general(a, b, (((1,), (1,)), ((), ())),
                           preferred_element_type=jnp.float32)


def _proj_kernel(x_ref, pos_ref, gpre_ref, wlat_ref, wkr_ref, wconv_ref, qn_ref, kvn_ref, wq_ref,
                 wk_ref, wv_ref, vone_ref, cw_ref, cn_ref, invf_ref,
                 q_ref, k_ref, v_ref, c_ref, utail_ref):
    tm = x_ref.shape[1]
    bf16 = jnp.bfloat16
    rg = tm // PROJ_ROW_GROUPS

    @pl.when(pl.program_id(1) == 0)
    def _():
        utail_ref[...] = jnp.zeros_like(utail_ref)

    def rows_pipeline(rows):
        h = _rms(x_ref[0, rows, :], gpre_ref[...]).astype(bf16)
        yield
        z_lat = _dot(h, wlat_ref[...])
        z_kr = _dot(h, wkr_ref[...])
        z_conv = _dot(h, wconv_ref[...])
        yield
        cq = _rms(z_lat[:, :Q_RANK], qn_ref[...]).astype(bf16)
        qq = _dot(cq, wq_ref[...])
        ckv = _rms(z_lat[:, Q_RANK:], kvn_ref[...]).astype(bf16)
        kk = _dot(ckv, wk_ref[...])
        v_ref[0, :, rows] = (_dot_nt(wv_ref[...], ckv) + vone_ref[...]).astype(bf16)

        ang_t = invf_ref[...] * pos_ref[0, :, rows]
        cos_t = jnp.cos(ang_t)
        sin_t = jnp.sin(ang_t)
        one_t = jnp.ones((HALF_LANES - HALF_ROPE, rg), jnp.float32)
        zero_t = jnp.zeros((HALF_LANES - HALF_ROPE, rg), jnp.float32)
        cos = jnp.concatenate([cos_t, one_t, cos_t, one_t], axis=0).T
        sin = jnp.concatenate([-sin_t, zero_t, sin_t, zero_t], axis=0).T
        scale = (NOPE_DIM + ROPE_DIM) ** -0.5 * LOG2_E
        cos_q = cos * scale
        sin_q = sin * scale
        kr = z_kr * cos + pltpu.roll(z_kr, HALF_LANES, axis=1) * sin

        bg = z_conv[:, :CONV_WIDTH]
        u = z_conv[:, CONV_WIDTH:2 * CONV_WIDTH] * z_conv[:, 2 * CONV_WIDTH:]
        tail = utail_ref[...]
        row = lax.broadcasted_iota(jnp.int32, (rg, CONV_WIDTH), 0)
        u1 = pltpu.roll(u, 1, axis=0)
        u1 = jnp.where(row == 0, tail[7:8, :], u1)
        u2 = pltpu.roll(u, 2, axis=0)
        u2 = jnp.where(row == 0, tail[6:7, :], jnp.where(row == 1, tail[7:8, :], u2))
        utail_ref[...] = u[rg - 8:rg, :]
        cw = cw_ref[...]
        y = cw[0:1, :] * u + cw[1:2, :] * u1 + cw[2:3, :] * u2
        cc = bg * y
        c_ref[0, rows, :] = _rms(cc, cn_ref[...]).astype(bf16)
        yield

        for hh in range(N_HEADS):
            lo = hh * LANES
            qh = qq[:, lo:lo + LANES]
            qh = qh * cos_q + pltpu.roll(qh, HALF_LANES, axis=1) * sin_q
            pair, ps = divmod(hh, HEADS_PER_STEP)
            pl_ = slice(ps * LANES, (ps + 1) * LANES)
            q_ref[0, pair, rows, pl_] = qh.astype(bf16)
            k_ref[0, pair, rows, pl_] = (kk[:, lo:lo + LANES] + kr).astype(bf16)

    live = []
    for r in range(PROJ_ROW_GROUPS):
        live.insert(0, rows_pipeline(slice(r * rg, (r + 1) * rg)))
        for _ in range(2):
            live = [g for g in live if next(g, _DONE) is not _DONE]
    while live:
        live = [g for g in live if next(g, _DONE) is not _DONE]


def _const_spec(shape):
    nd = len(shape)
    return pl.BlockSpec(shape, lambda *_: (0,) * nd, pipeline_mode=pl.Buffered(1))


def _proj_call(x, posf, gpre, wlat, wkr, wconv, qn, kvn, wq, wk, wv, vone, cw, cn, invf):
    B, S, _ = x.shape
    tm = PROJ_TM
    grid = (B, S // tm)
    tok = lambda w: pl.BlockSpec((1, tm, w), lambda b, s: (b, s, 0))
    pos_spec = pl.BlockSpec((1, 1, tm), lambda b, s: (b, 0, s))
    qk_spec = pl.BlockSpec((1, N_PAIRS, tm, PAIR_LANES), lambda b, s: (b, 0, s, 0))
    consts = (gpre, wlat, wkr, wconv, qn, kvn, wq, wk, wv, vone, cw, cn, invf)
    bf16 = jnp.bfloat16
    return pl.pallas_call(
        _proj_kernel,
        grid=grid,
        in_specs=[tok(D_MODEL), pos_spec] + [_const_spec(c.shape) for c in consts],
        out_specs=[qk_spec, qk_spec,
                   pl.BlockSpec((1, VT_WIDTH, tm), lambda b, s: (b, 0, s)),
                   tok(CONV_WIDTH)],
        out_shape=[jax.ShapeDtypeStruct((B, N_PAIRS, S, PAIR_LANES), bf16),
                   jax.ShapeDtypeStruct((B, N_PAIRS, S, PAIR_LANES), bf16),
                   jax.ShapeDtypeStruct((B, VT_WIDTH, S), bf16),
                   jax.ShapeDtypeStruct((B, S, CONV_WIDTH), bf16)],
        scratch_shapes=[pltpu.VMEM((8, CONV_WIDTH), jnp.float32)],
        compiler_params=pltpu.CompilerParams(
            dimension_semantics=("arbitrary", "arbitrary"),
            vmem_limit_bytes=VMEM_LIMIT),
        name="proj",
    )(x, posf, *consts)


def _attn_kernel(q_ref, k_ref, vt_ref, o_ref, s_scr, p_scr):
    S = q_ref.shape[2]
    tq = ATTN_TQ
    bf16 = jnp.bfloat16
    neg = jnp.finfo(jnp.float32).min
    key = lax.broadcasted_iota(jnp.int32, (tq, tq), 0)
    qry = lax.broadcasted_iota(jnp.int32, (tq, tq), 1)
    causal = key <= qry
    nt = S // tq
    tiles = list(range(0, nt, 2)) + list(range(nt - 1 - nt % 2, 0, -2))
    bodies = [(j, i, hh) for j in range(PAIRS_PER_TRIP) for i in tiles
              for hh in range(HEADS_PER_STEP)]

    def chunk(c):
        return slice(c * tq, (c + 1) * tq)

    def one_trip(trip, carry):
        col_max = {}
        head_out = {}
        pace = {"tok": jnp.zeros((8, tq), jnp.float32)}

        def scores(n):
            j, i, hh = bodies[n]
            pair = trip * PAIRS_PER_TRIP + j
            hs = slice(hh * LANES, (hh + 1) * LANES)
            q = q_ref[0, pair, chunk(i), hs]
            part = None
            per = -(-(i + 1) // QK_STREAMS)
            starts = list(range(0, i + 1, per))
            streams = [_dot_nt(k_ref[0, pair, c0 * tq:min(c0 + per, i + 1) * tq, hs], q)
                       for c0 in starts]
            order = [c0 + j for j in range(per) for c0 in starts
                     if c0 + j <= min(c0 + per - 1, i)]
            for pos, c in enumerate(order):
                s = streams[c // per][chunk(c % per), :]
                if c == i:
                    s = jnp.where(causal, s, neg)
                s_scr[n % 2, chunk(c), :] = s
                cm = jnp.max(s.reshape(tq // 8, 8, tq), axis=0)
                part = cm if part is None else jnp.maximum(part, cm)
                pace["tok"] = cm
                if pos == len(order) - 1:
                    col_max[n] = jnp.max(part, axis=0, keepdims=True)
                yield

        def exps(n):
            _, i, _ = bodies[n]
            m = col_max.pop(n)
            for c in range(i + 1):
                m_c = m + _dependent_zero(pace["tok"])[0:1, :]
                p_scr[n % 2, chunk(c), :] = jnp.exp2(
                    (s_scr[n % 2, chunk(c), :] - m_c).astype(bf16))
                yield

        def values(n):
            j, i, hh = bodies[n]
            pair = trip * PAIRS_PER_TRIP + j
            acc = None
            for c in range(i + 1):
                vt = vt_ref[0, pair, hh * V_ROWS:(hh + 1) * V_ROWS, chunk(c)]
                d = _dot(vt, p_scr[n % 2, chunk(c), :])
                acc = d if acc is None else acc + d
                if c == i:
                    head_out[hh] = acc[0:V_DIM, :] * (1.0 / acc[V_DIM:V_DIM + 1, :])
                    if hh == HEADS_PER_STEP - 1:
                        o_ref[0, pair, chunk(i), :] = jnp.concatenate(
                            [head_out.pop(h) for h in range(HEADS_PER_STEP)], axis=0).T
                yield

        for step in range(len(bodies) + 2):
            live = []
            if step < len(bodies):
                live.append(scores(step))
            if 0 <= step - 1 < len(bodies):
                live.append(exps(step - 1))
            if 0 <= step - 2 < len(bodies):
                live.append(values(step - 2))
            while live:
                live = [g for g in live if next(g, _DONE) is not _DONE]
        return carry

    lax.fori_loop(0, q_ref.shape[1] // PAIRS_PER_TRIP, one_trip, 0)


def _attn_call(q, k, vt):
    B, _, S, _ = q.shape
    vt = vt.reshape(B, N_PAIRS, HEADS_PER_STEP * V_ROWS, S)
    grid = (B,)
    qk_spec = pl.BlockSpec((1, N_PAIRS, S, PAIR_LANES), lambda b: (b, 0, 0, 0))
    vt_spec = pl.BlockSpec((1, N_PAIRS, HEADS_PER_STEP * V_ROWS, S), lambda b: (b, 0, 0, 0))
    o_spec = pl.BlockSpec((1, N_PAIRS, S, HEADS_PER_STEP * V_DIM), lambda b: (b, 0, 0, 0))
    return pl.pallas_call(
        _attn_kernel,
        grid=grid,
        in_specs=[qk_spec, qk_spec, vt_spec],
        out_specs=o_spec,
        out_shape=jax.ShapeDtypeStruct((B, N_PAIRS, S, HEADS_PER_STEP * V_DIM), jnp.float32),
        scratch_shapes=[pltpu.VMEM((2, S, ATTN_TQ), jnp.float32),
                        pltpu.VMEM((2, S, ATTN_TQ), jnp.bfloat16)],
        compiler_params=pltpu.CompilerParams(
            dimension_semantics=("arbitrary",),
            vmem_limit_bytes=VMEM_LIMIT),
        name="attn",
    )(q, k, vt)


def _post_kernel(x_ref, a_ref, c_ref, p_ref, an_ref, wo_ref, mpn_ref, fpre_ref,
                 wg_ref, wu_ref, wd_ref, fpost_ref, wpp_ref, pn_ref, wpg_ref,
                 o_ref):
    bf16 = jnp.bfloat16
    rg = x_ref.shape[0] // POST_ROW_GROUPS

    def rows_pipeline(rows):
        a = jnp.concatenate([a_ref[0, j, rows, :] for j in range(N_PAIRS)], axis=-1)
        a = _rms(a, an_ref[...]).astype(bf16)
        m = jnp.concatenate([a, c_ref[rows, :]], axis=-1)
        y = _dot(m, wo_ref[...])
        yield
        x1 = x_ref[rows, :] + _rms(y, mpn_ref[...])
        h = _rms(x1, fpre_ref[...]).astype(bf16)
        yield
        f = None
        for j in range(D_FF // FF_CHUNK):
            cs = slice(j * FF_CHUNK, (j + 1) * FF_CHUNK)
            g = _dot(h, wg_ref[:, cs])
            u = _dot(h, wu_ref[:, cs])
            act = (g * jax.nn.sigmoid(g) * u).astype(bf16)
            part = _dot(act, wd_ref[cs, :])
            f = part if f is None else f + part
            yield
        x2 = x1 + _rms(f, fpost_ref[...])
        e = _rms(_dot(p_ref[rows, :].astype(bf16), wpp_ref[...]), pn_ref[...])
        gate = jax.nn.sigmoid(_dot(x2.astype(bf16), wpg_ref[...]))
        o_ref[rows, :] = x2 + gate * e

    live = []
    for r in range(POST_ROW_GROUPS):
        live.insert(0, rows_pipeline(slice(r * rg, (r + 1) * rg)))
        for _ in range(2):
            live = [g for g in live if next(g, _DONE) is not _DONE]
    while live:
        live = [g for g in live if next(g, _DONE) is not _DONE]


def _post_call(x2d, a4d, c2d, p2d, an, wo, mpn, fpre, wg, wu, wd, fpost, wpp, pn, wpg):
    T = x2d.shape[0]
    tm = POST_TM
    tiles_per_row = a4d.shape[2] // tm
    tok = lambda w: pl.BlockSpec((tm, w), lambda t: (t, 0))
    a_spec = pl.BlockSpec((1, N_PAIRS, tm, a4d.shape[3]),
                          lambda t: (t // tiles_per_row, 0, t % tiles_per_row, 0))
    consts = (an, wo, mpn, fpre, wg, wu, wd, fpost, wpp, pn, wpg)
    return pl.pallas_call(
        _post_kernel,
        grid=(T // tm,),
        in_specs=[tok(D_MODEL), a_spec, tok(CONV_WIDTH), tok(PLE_DIM)]
        + [_const_spec(c.shape) for c in consts],
        out_specs=tok(D_MODEL),
        out_shape=jax.ShapeDtypeStruct((T, D_MODEL), jnp.float32),
        compiler_params=pltpu.CompilerParams(
            dimension_semantics=("arbitrary",),
            vmem_limit_bytes=POST_VMEM_LIMIT),
        name="post",
    )(x2d, a4d, c2d, p2d, *consts)


def _inv_freq_column():
    inv_freq = 1.0 / (ROPE_THETA ** (np.arange(HALF_ROPE, dtype=np.float32) / HALF_ROPE))
    return jnp.asarray(inv_freq.astype(np.float32).reshape(HALF_ROPE, 1))


def _head_lanes(nope, rope):
    ref = nope if nope is not None else rope
    lead = ref.shape[:-1]
    zeros = lambda n: jnp.zeros(lead + (n,), ref.dtype)
    n_lo = HALF_LANES - HALF_ROPE
    parts = [
        rope[..., :HALF_ROPE] if rope is not None else zeros(HALF_ROPE),
        nope[..., :n_lo] if nope is not None else zeros(n_lo),
        rope[..., HALF_ROPE:] if rope is not None else zeros(HALF_ROPE),
        nope[..., n_lo:] if nope is not None else zeros(NOPE_DIM - n_lo),
        zeros(LANES - NOPE_DIM - ROPE_DIM),
    ]
    return jnp.concatenate(parts, axis=-1)


def _prep_w_in(w_in):
    o2 = Q_RANK + KV_RANK
    o3 = o2 + ROPE_DIM
    bf16 = jnp.bfloat16
    kr = _head_lanes(None, w_in[:, o2:o3])
    return w_in[:, :o2].astype(bf16), kr.astype(bf16), w_in[:, o3:].astype(bf16)


def _prep_w_uq(w_uq):
    w = w_uq.reshape(Q_RANK, N_HEADS, NOPE_DIM + ROPE_DIM)
    wq = _head_lanes(w[..., :NOPE_DIM], w[..., NOPE_DIM:])
    return wq.reshape(Q_RANK, QK_WIDTH).astype(jnp.bfloat16)


def _prep_w_ukv(w_ukv):
    w = w_ukv.reshape(KV_RANK, N_HEADS, NOPE_DIM + V_DIM)
    k_nope, v = w[..., :NOPE_DIM], w[..., NOPE_DIM:]
    wk = _head_lanes(k_nope, None).reshape(KV_RANK, QK_WIDTH)
    v_pad = jnp.concatenate([v, jnp.zeros((KV_RANK, N_HEADS, SUM_ROWS), w.dtype)], axis=-1)
    wv_t = v_pad.reshape(KV_RANK, VT_WIDTH).T
    return wk.astype(jnp.bfloat16), wv_t.astype(jnp.bfloat16)


def _ones_rows_column():
    col = np.zeros((N_HEADS, V_ROWS, 1), np.float32)
    col[:, V_DIM:, :] = 1.0
    return jnp.asarray(col.reshape(VT_WIDTH, 1))


def kernel(x, p, positions, mix_pre_norm, w_in, q_norm, kv_norm, w_uq, w_ukv, conv_w, attn_group_norm, conv_group_norm, w_out, mix_post_norm, ffn_pre_norm, w_gate, w_up, w_down, ffn_post_norm, w_ple_proj, ple_norm, w_ple_gate):
    B, S, D = x.shape
    depth = w_in.shape[0]
    bf16 = jnp.bfloat16
    row = lambda g: g.reshape(1, -1)
    invf = _inv_freq_column()
    vone = _ones_rows_column()
    posf = positions.astype(jnp.float32)[:, None, :]
    for i in range(depth):
        wk, wv = _prep_w_ukv(w_ukv[i])
        q, k, v, c = _proj_call(
            x, posf, row(mix_pre_norm[i]), *_prep_w_in(w_in[i]), row(q_norm[i]),
            row(kv_norm[i]), _prep_w_uq(w_uq[i]), wk, wv, vone, conv_w[i],
            row(conv_group_norm[i]), invf)
        a = _attn_call(q, k, v)
        out = _post_call(
            x.reshape(B * S, D), a,
            c.reshape(B * S, CONV_WIDTH), p[i].reshape(B * S, PLE_DIM),
            row(attn_group_norm[i]), w_out[i].astype(bf16), row(mix_post_norm[i]),
            row(ffn_pre_norm[i]), w_gate[i].astype(bf16), w_up[i].astype(bf16),
            w_down[i].astype(bf16), row(ffn_post_norm[i]),
            w_ple_proj[i].astype(bf16), row(ple_norm[i]), w_ple_gate[i].astype(bf16))
        x = out.reshape(B, S, D)
    return x
```

```python
import numpy as np
import jax
import jax.numpy as jnp
from jax import lax
from jax.experimental import pallas as pl
from jax.experimental.pallas import tpu as pltpu

D_MODEL = 1024
PLE_DIM = 256
ATTN_WIDTH = 512
CONV_WIDTH = 512
V_DIM = 64
NOPE_DIM = 64
ROPE_DIM = 32
HALF_ROPE = ROPE_DIM // 2
N_HEADS = 8
Q_RANK = 256
KV_RANK = 256
CONV_K = 3
D_FF = 2816
ROPE_THETA = 10000.0
EPS = 1e-6
LOG2_E = 1.4426950408889634

LANES = 128
HALF_LANES = LANES // 2
QK_WIDTH = N_HEADS * LANES
HEADS_PER_STEP = 2
N_PAIRS = N_HEADS // HEADS_PER_STEP
PAIR_LANES = HEADS_PER_STEP * LANES
SUM_ROWS = 16
V_ROWS = V_DIM + SUM_ROWS
VT_WIDTH = N_HEADS * V_ROWS
FF_CHUNK = 256

PROJ_TM = 1024
PROJ_ROW_GROUPS = 2
ATTN_TQ = 256
QK_STREAMS = 4
PAIRS_PER_TRIP = 2
POST_TM = 1024
POST_ROW_GROUPS = 2
VMEM_LIMIT = 56 * 1024 * 1024
POST_VMEM_LIMIT = 60 * 1024 * 1024

_DONE = object()


def _rms(x, g):
    ms = jnp.mean(x * x, axis=-1, keepdims=True)
    return x * lax.rsqrt(ms + EPS) * g


def _dot(a, b):
    return jnp.dot(a, b, preferred_element_type=jnp.float32)


def _dependent_zero(x):
    bits = lax.bitcast_convert_type(x, jnp.int32)
    bits = lax.shift_right_logical(lax.shift_right_logical(bits, 16), 16)
    return bits.astype(jnp.float32)


def _dot_nt(a, b):
    return lax.dot_general(a, b, (((1,), (1,)), ((), ())),
                           preferred_element_type=jnp.float32)


def _proj_kernel(x_ref, pos_ref, gpre_ref, wlat_ref, wkr_ref, wconv_ref, qn_ref, kvn_ref, wq_ref,
                 wk_ref, wv_ref, vone_ref, cw_ref, cn_ref, invf_ref,
                 q_ref, k_ref, v_ref, c_ref, utail_ref):
    tm = x_ref.shape[1]
    bf16 = jnp.bfloat16
    rg = tm // PROJ_ROW_GROUPS

    @pl.when(pl.program_id(1) == 0)
    def _():
        utail_ref[...] = jnp.zeros_like(utail_ref)

    def rows_pipeline(rows):
        h = _rms(x_ref[0, rows, :], gpre_ref[...]).astype(bf16)
        yield
        z_lat = _dot(h, wlat_ref[...])
        z_kr = _dot(h, wkr_ref[...])
        z_conv = _dot(h, wconv_ref[...])
        yield
        cq = _rms(z_lat[:, :Q_RANK], qn_ref[...]).astype(bf16)
        qq = _dot(cq, wq_ref[...])
        ckv = _rms(z_lat[:, Q_RANK:], kvn_ref[...]).astype(bf16)
        kk = _dot(ckv, wk_ref[...])
        v_ref[0, :, rows] = (_dot_nt(wv_ref[...], ckv) + vone_ref[...]).astype(bf16)

        ang_t = invf_ref[...] * pos_ref[0, :, rows]
        cos_t = jnp.cos(ang_t)
        sin_t = jnp.sin(ang_t)
        one_t = jnp.ones((HALF_LANES - HALF_ROPE, rg), jnp.float32)
        zero_t = jnp.zeros((HALF_LANES - HALF_ROPE, rg), jnp.float32)
        cos = jnp.concatenate([cos_t, one_t, cos_t, one_t], axis=0).T
        sin = jnp.concatenate([-sin_t, zero_t, sin_t, zero_t], axis=0).T
        scale = (NOPE_DIM + ROPE_DIM) ** -0.5 * LOG2_E
        cos_q = cos * scale
        sin_q = sin * scale
        kr = z_kr * cos + pltpu.roll(z_kr, HALF_LANES, axis=1) * sin

        bg = z_conv[:, :CONV_WIDTH]
        u = z_conv[:, CONV_WIDTH:2 * CONV_WIDTH] * z_conv[:, 2 * CONV_WIDTH:]
        tail = utail_ref[...]
        row = lax.broadcasted_iota(jnp.int32, (rg, CONV_WIDTH), 0)
        u1 = pltpu.roll(u, 1, axis=0)
        u1 = jnp.where(row == 0, tail[7:8, :], u1)
        u2 = pltpu.roll(u, 2, axis=0)
        u2 = jnp.where(row == 0, tail[6:7, :], jnp.where(row == 1, tail[7:8, :], u2))
        utail_ref[...] = u[rg - 8:rg, :]
        cw = cw_ref[...]
        y = cw[0:1, :] * u + cw[1:2, :] * u1 + cw[2:3, :] * u2
        cc = bg * y
        c_ref[0, rows, :] = _rms(cc, cn_ref[...]).astype(bf16)
        yield

        for hh in range(N_HEADS):
            lo = hh * LANES
            qh = qq[:, lo:lo + LANES]
            qh = qh * cos_q + pltpu.roll(qh, HALF_LANES, axis=1) * sin_q
            pair, ps = divmod(hh, HEADS_PER_STEP)
            pl_ = slice(ps * LANES, (ps + 1) * LANES)
            q_ref[0, pair, rows, pl_] = qh.astype(bf16)
            k_ref[0, pair, rows, pl_] = (kk[:, lo:lo + LANES] + kr).astype(bf16)

    live = []
    for r in range(PROJ_ROW_GROUPS):
        live.insert(0, rows_pipeline(slice(r * rg, (r + 1) * rg)))
        for _ in range(2):
            live = [g for g in live if next(g, _DONE) is not _DONE]
    while live:
        live = [g for g in live if next(g, _DONE) is not _DONE]


def _const_spec(shape):
    nd = len(shape)
    return pl.BlockSpec(shape, lambda *_: (0,) * nd, pipeline_mode=pl.Buffered(1))


def _proj_call(x, posf, gpre, wlat, wkr, wconv, qn, kvn, wq, wk, wv, vone, cw, cn, invf):
    B, S, _ = x.shape
    tm = PROJ_TM
    grid = (B, S // tm)
    tok = lambda w: pl.BlockSpec((1, tm, w), lambda b, s: (b, s, 0))
    pos_spec = pl.BlockSpec((1, 1, tm), lambda b, s: (b, 0, s))
    qk_spec = pl.BlockSpec((1, N_PAIRS, tm, PAIR_LANES), lambda b, s: (b, 0, s, 0))
    consts = (gpre, wlat, wkr, wconv, qn, kvn, wq, wk, wv, vone, cw, cn, invf)
    bf16 = jnp.bfloat16
    return pl.pallas_call(
        _proj_kernel,
        grid=grid,
        in_specs=[tok(D_MODEL), pos_spec] + [_const_spec(c.shape) for c in consts],
        out_specs=[qk_spec, qk_spec,
                   pl.BlockSpec((1, VT_WIDTH, tm), lambda b, s: (b, 0, s)),
                   tok(CONV_WIDTH)],
        out_shape=[jax.ShapeDtypeStruct((B, N_PAIRS, S, PAIR_LANES), bf16),
                   jax.ShapeDtypeStruct((B, N_PAIRS, S, PAIR_LANES), bf16),
                   jax.ShapeDtypeStruct((B, VT_WIDTH, S), bf16),
                   jax.ShapeDtypeStruct((B, S, CONV_WIDTH), bf16)],
        scratch_shapes=[pltpu.VMEM((8, CONV_WIDTH), jnp.float32)],
        compiler_params=pltpu.CompilerParams(
            dimension_semantics=("arbitrary", "arbitrary"),
            vmem_limit_bytes=VMEM_LIMIT),
        name="proj",
    )(x, posf, *consts)


def _attn_kernel(q_ref, k_ref, vt_ref, o_ref, s_scr, p_scr):
    S = q_ref.shape[2]
    tq = ATTN_TQ
    bf16 = jnp.bfloat16
    neg = jnp.finfo(jnp.float32).min
    key = lax.broadcasted_iota(jnp.int32, (tq, tq), 0)
    qry = lax.broadcasted_iota(jnp.int32, (tq, tq), 1)
    causal = key <= qry
    nt = S // tq
    bodies = [(j, i, hh) for j in range(PAIRS_PER_TRIP)
              for i in (range(nt) if j % 2 == 0 else range(nt - 1, -1, -1))
              for hh in range(HEADS_PER_STEP)]

    def chunk(c):
        return slice(c * tq, (c + 1) * tq)

    def one_trip(trip, carry):
        col_max = {}
        head_out = {}
        pace = {"tok": jnp.zeros((8, tq), jnp.float32)}

        def scores(n):
            j, i, hh = bodies[n]
            pair = trip * PAIRS_PER_TRIP + j
            hs = slice(hh * LANES, (hh + 1) * LANES)
            q = q_ref[0, pair, chunk(i), hs]
            part = None
            per = -(-(i + 1) // QK_STREAMS)
            starts = list(range(0, i + 1, per))
            streams = [_dot_nt(k_ref[0, pair, c0 * tq:min(c0 + per, i + 1) * tq, hs], q)
                       for c0 in starts]
            order = [c0 + j for j in range(per) for c0 in starts
                     if c0 + j <= min(c0 + per - 1, i)]
            for pos, c in enumerate(order):
                s = streams[c // per][chunk(c % per), :]
                if c == i:
                    s = jnp.where(causal, s, neg)
                s_scr[n % 2, chunk(c), :] = s
                cm = jnp.max(s.reshape(tq // 8, 8, tq), axis=0)
                part = cm if part is None else jnp.maximum(part, cm)
                pace["tok"] = cm
                if pos == len(order) - 1:
                    col_max[n] = jnp.max(part, axis=0, keepdims=True)
                yield

        def exps(n):
            _, i, _ = bodies[n]
            m = col_max.pop(n)
            for c in range(i + 1):
                m_c = m + _dependent_zero(pace["tok"])[0:1, :]
                p_scr[n % 2, chunk(c), :] = jnp.exp2(
                    (s_scr[n % 2, chunk(c), :] - m_c).astype(bf16))
                yield

        def values(n):
            j, i, hh = bodies[n]
            pair = trip * PAIRS_PER_TRIP + j
            acc = None
            for c in range(i + 1):
                vt = vt_ref[0, pair, hh * V_ROWS:(hh + 1) * V_ROWS, chunk(c)]
                d = _dot(vt, p_scr[n % 2, chunk(c), :])
                acc = d if acc is None else acc + d
                if c == i:
                    head_out[hh] = acc[0:V_DIM, :] * (1.0 / acc[V_DIM:V_DIM + 1, :])
                    if hh == HEADS_PER_STEP - 1:
                        o_ref[0, pair, chunk(i), :] = jnp.concatenate(
                            [head_out.pop(h) for h in range(HEADS_PER_STEP)], axis=0).T
                yield

        for step in range(len(bodies) + 2):
            live = []
            if step < len(bodies):
                live.append(scores(step))
            if 0 <= step - 1 < len(bodies):
                live.append(exps(step - 1))
            if 0 <= step - 2 < len(bodies):
                live.append(values(step - 2))
            while live:
                live = [g for g in live if next(g, _DONE) is not _DONE]
        return carry

    lax.fori_loop(0, q_ref.shape[1] // PAIRS_PER_TRIP, one_trip, 0)


def _attn_call(q, k, vt):
    B, _, S, _ = q.shape
    vt = vt.reshape(B, N_PAIRS, HEADS_PER_STEP * V_ROWS, S)
    grid = (B,)
    qk_spec = pl.BlockSpec((1, N_PAIRS, S, PAIR_LANES), lambda b: (b, 0, 0, 0))
    vt_spec = pl.BlockSpec((1, N_PAIRS, HEADS_PER_STEP * V_ROWS, S), lambda b: (b, 0, 0, 0))
    o_spec = pl.BlockSpec((1, N_PAIRS, S, HEADS_PER_STEP * V_DIM), lambda b: (b, 0, 0, 0))
    return pl.pallas_call(
        _attn_kernel,
        grid=grid,
        in_specs=[qk_spec, qk_spec, vt_spec],
        out_specs=o_spec,
        out_shape=jax.ShapeDtypeStruct((B, N_PAIRS, S, HEADS_PER_STEP * V_DIM), jnp.float32),
        scratch_shapes=[pltpu.VMEM((2, S, ATTN_TQ), jnp.float32),
                        pltpu.VMEM((2, S, ATTN_TQ), jnp.bfloat16)],
        compiler_params=pltpu.CompilerParams(
            dimension_semantics=("arbitrary",),
            vmem_limit_bytes=VMEM_LIMIT),
        name="attn",
    )(q, k, vt)


def _post_kernel(x_ref, a_ref, c_ref, p_ref, an_ref, wo_ref, mpn_ref, fpre_ref,
                 wg_ref, wu_ref, wd_ref, fpost_ref, wpp_ref, pn_ref, wpg_ref,
                 o_ref):
    bf16 = jnp.bfloat16
    rg = x_ref.shape[0] // POST_ROW_GROUPS

    def rows_pipeline(rows):
        a = jnp.concatenate([a_ref[0, j, rows, :] for j in range(N_PAIRS)], axis=-1)
        a = _rms(a, an_ref[...]).astype(bf16)
        m = jnp.concatenate([a, c_ref[rows, :]], axis=-1)
        y = _dot(m, wo_ref[...])
        yield
        x1 = x_ref[rows, :] + _rms(y, mpn_ref[...])
        h = _rms(x1, fpre_ref[...]).astype(bf16)
        yield
        f = None
        for j in range(D_FF // FF_CHUNK):
            cs = slice(j * FF_CHUNK, (j + 1) * FF_CHUNK)
            g = _dot(h, wg_ref[:, cs])
            u = _dot(h, wu_ref[:, cs])
            act = (g * jax.nn.sigmoid(g) * u).astype(bf16)
            part = _dot(act, wd_ref[cs, :])
            f = part if f is None else f + part
            yield
        x2 = x1 + _rms(f, fpost_ref[...])
        e = _rms(_dot(p_ref[rows, :].astype(bf16), wpp_ref[...]), pn_ref[...])
        gate = jax.nn.sigmoid(_dot(x2.astype(bf16), wpg_ref[...]))
        o_ref[rows, :] = x2 + gate * e

    live = []
    for r in range(POST_ROW_GROUPS):
        live.insert(0, rows_pipeline(slice(r * rg, (r + 1) * rg)))
        for _ in range(2):
            live = [g for g in live if next(g, _DONE) is not _DONE]
    while live:
        live = [g for g in live if next(g, _DONE) is not _DONE]


def _post_call(x2d, a4d, c2d, p2d, an, wo, mpn, fpre, wg, wu, wd, fpost, wpp, pn, wpg):
    T = x2d.shape[0]
    tm = POST_TM
    tiles_per_row = a4d.shape[2] // tm
    tok = lambda w: pl.BlockSpec((tm, w), lambda t: (t, 0))
    a_spec = pl.BlockSpec((1, N_PAIRS, tm, a4d.shape[3]),
                          lambda t: (t // tiles_per_row, 0, t % tiles_per_row, 0))
    consts = (an, wo, mpn, fpre, wg, wu, wd, fpost, wpp, pn, wpg)
    return pl.pallas_call(
        _post_kernel,
        grid=(T // tm,),
        in_specs=[tok(D_MODEL), a_spec, tok(CONV_WIDTH), tok(PLE_DIM)]
        + [_const_spec(c.shape) for c in consts],
        out_specs=tok(D_MODEL),
        out_shape=jax.ShapeDtypeStruct((T, D_MODEL), jnp.float32),
        compiler_params=pltpu.CompilerParams(
            dimension_semantics=("arbitrary",),
            vmem_limit_bytes=POST_VMEM_LIMIT),
        name="post",
    )(x2d, a4d, c2d, p2d, *consts)


def _inv_freq_column():
    inv_freq = 1.0 / (ROPE_THETA ** (np.arange(HALF_ROPE, dtype=np.float32) / HALF_ROPE))
    return jnp.asarray(inv_freq.astype(np.float32).reshape(HALF_ROPE, 1))


def _head_lanes(nope, rope):
    ref = nope if nope is not None else rope
    lead = ref.shape[:-1]
    zeros = lambda n: jnp.zeros(lead + (n,), ref.dtype)
    n_lo = HALF_LANES - HALF_ROPE
    parts = [
        rope[..., :HALF_ROPE] if rope is not None else zeros(HALF_ROPE),
        nope[..., :n_lo] if nope is not None else zeros(n_lo),
        rope[..., HALF_ROPE:] if rope is not None else zeros(HALF_ROPE),
        nope[..., n_lo:] if nope is not None else zeros(NOPE_DIM - n_lo),
        zeros(LANES - NOPE_DIM - ROPE_DIM),
    ]
    return jnp.concatenate(parts, axis=-1)


def _prep_w_in(w_in):
    o2 = Q_RANK + KV_RANK
    o3 = o2 + ROPE_DIM
    bf16 = jnp.bfloat16
    kr = _head_lanes(None, w_in[:, o2:o3])
    return w_in[:, :o2].astype(bf16), kr.astype(bf16), w_in[:, o3:].astype(bf16)


def _prep_w_uq(w_uq):
    w = w_uq.reshape(Q_RANK, N_HEADS, NOPE_DIM + ROPE_DIM)
    wq = _head_lanes(w[..., :NOPE_DIM], w[..., NOPE_DIM:])
    return wq.reshape(Q_RANK, QK_WIDTH).astype(jnp.bfloat16)


def _prep_w_ukv(w_ukv):
    w = w_ukv.reshape(KV_RANK, N_HEADS, NOPE_DIM + V_DIM)
    k_nope, v = w[..., :NOPE_DIM], w[..., NOPE_DIM:]
    wk = _head_lanes(k_nope, None).reshape(KV_RANK, QK_WIDTH)
    v_pad = jnp.concatenate([v, jnp.zeros((KV_RANK, N_HEADS, SUM_ROWS), w.dtype)], axis=-1)
    wv_t = v_pad.reshape(KV_RANK, VT_WIDTH).T
    return wk.astype(jnp.bfloat16), wv_t.astype(jnp.bfloat16)


def _ones_rows_column():
    col = np.zeros((N_HEADS, V_ROWS, 1), np.float32)
    col[:, V_DIM:, :] = 1.0
    return jnp.asarray(col.reshape(VT_WIDTH, 1))


def kernel(x, p, positions, mix_pre_norm, w_in, q_norm, kv_norm, w_uq, w_ukv, conv_w, attn_group_norm, conv_group_norm, w_out, mix_post_norm, ffn_pre_norm, w_gate, w_up, w_down, ffn_post_norm, w_ple_proj, ple_norm, w_ple_gate):
    B, S, D = x.shape
    depth = w_in.shape[0]
    bf16 = jnp.bfloat16
    row = lambda g: g.reshape(1, -1)
    invf = _inv_freq_column()
    vone = _ones_rows_column()
    posf = positions.astype(jnp.float32)[:, None, :]
    for i in range(depth):
        wk, wv = _prep_w_ukv(w_ukv[i])
        q, k, v, c = _proj_call(
            x, posf, row(mix_pre_norm[i]), *_prep_w_in(w_in[i]), row(q_norm[i]),
            row(kv_norm[i]), _prep_w_uq(w_uq[i]), wk, wv, vone, conv_w[i],
            row(conv_group_norm[i]), invf)
        a = _attn_call(q, k, v)
        out = _post_call(
            x.reshape(B * S, D), a,
            c.reshape(B * S, CONV_WIDTH), p[i].reshape(B * S, PLE_DIM),
            row(attn_group_norm[i]), w_out[i].astype(bf16), row(mix_post_norm[i]),
            row(ffn_pre_norm[i]), w_gate[i].astype(bf16), w_up[i].astype(bf16),
            w_down[i].astype(bf16), row(ffn_post_norm[i]),
            w_ple_proj[i].astype(bf16), row(ple_norm[i]), w_ple_gate[i].astype(bf16))
        x = out.reshape(B, S, D)
    return x
```

```python
import numpy as np
import jax
import jax.numpy as jnp
from jax import lax
from jax.experimental import pallas as pl
from jax.experimental.pallas import tpu as pltpu

D_MODEL = 1024
PLE_DIM = 256
ATTN_WIDTH = 512
CONV_WIDTH = 512
V_DIM = 64
NOPE_DIM = 64
ROPE_DIM = 32
HALF_ROPE = ROPE_DIM // 2
N_HEADS = 8
Q_RANK = 256
KV_RANK = 256
CONV_K = 3
D_FF = 2816
ROPE_THETA = 10000.0
EPS = 1e-6
LOG2_E = 1.4426950408889634

LANES = 128
HALF_LANES = LANES // 2
QK_WIDTH = N_HEADS * LANES
HEADS_PER_STEP = 2
N_PAIRS = N_HEADS // HEADS_PER_STEP
PAIR_LANES = HEADS_PER_STEP * LANES
SUM_ROWS = 16
V_ROWS = V_DIM + SUM_ROWS
VT_WIDTH = N_HEADS * V_ROWS
FF_CHUNK = 256

PROJ_TM = 1024
PROJ_ROW_GROUPS = 2
ATTN_TQ = 256
QK_STREAMS = 4
PAIRS_PER_TRIP = 2
POST_TM = 1024
POST_ROW_GROUPS = 2
VMEM_LIMIT = 56 * 1024 * 1024
POST_VMEM_LIMIT = 60 * 1024 * 1024

_DONE = object()


def _rms(x, g):
    ms = jnp.mean(x * x, axis=-1, keepdims=True)
    return x * lax.rsqrt(ms + EPS) * g


def _dot(a, b):
    return jnp.dot(a, b, preferred_element_type=jnp.float32)


def _dependent_zero(x):
    bits = lax.bitcast_convert_type(x, jnp.int32)
    bits = lax.shift_right_logical(lax.shift_right_logical(bits, 16), 16)
    return bits.astype(jnp.float32)


def _dot_nt(a, b):
    return lax.dot_general(a, b, (((1,), (1,)), ((), ())),
                           preferred_element_type=jnp.float32)


def _proj_kernel(x_ref, pos_ref, gpre_ref, wlat_ref, wkr_ref, wconv_ref, qn_ref, kvn_ref, wq_ref,
                 wk_ref, wv_ref, vone_ref, cw_ref, cn_ref, invf_ref,
                 q_ref, k_ref, v_ref, c_ref, utail_ref):
    tm = x_ref.shape[1]
    bf16 = jnp.bfloat16
    rg = tm // PROJ_ROW_GROUPS

    @pl.when(pl.program_id(1) == 0)
    def _():
        utail_ref[...] = jnp.zeros_like(utail_ref)

    def rows_pipeline(rows):
        h = _rms(x_ref[0, rows, :], gpre_ref[...]).astype(bf16)
        yield
        z_lat = _dot(h, wlat_ref[...])
        z_kr = _dot(h, wkr_ref[...])
        z_conv = _dot(h, wconv_ref[...])
        yield
        cq = _rms(z_lat[:, :Q_RANK], qn_ref[...]).astype(bf16)
        qq = _dot(cq, wq_ref[...])
        ckv = _rms(z_lat[:, Q_RANK:], kvn_ref[...]).astype(bf16)
        kk = _dot(ckv, wk_ref[...])
        v_ref[0, :, rows] = (_dot_nt(wv_ref[...], ckv) + vone_ref[...]).astype(bf16)

        ang_t = invf_ref[...] * pos_ref[0, :, rows]
        cos_t = jnp.cos(ang_t)
        sin_t = jnp.sin(ang_t)
        one_t = jnp.ones((HALF_LANES - HALF_ROPE, rg), jnp.float32)
        zero_t = jnp.zeros((HALF_LANES - HALF_ROPE, rg), jnp.float32)
        cos = jnp.concatenate([cos_t, one_t, cos_t, one_t], axis=0).T
        sin = jnp.concatenate([-sin_t, zero_t, sin_t, zero_t], axis=0).T
        scale = (NOPE_DIM + ROPE_DIM) ** -0.5 * LOG2_E
        cos_q = cos * scale
        sin_q = sin * scale
        kr = z_kr * cos + pltpu.roll(z_kr, HALF_LANES, axis=1) * sin

        bg = z_conv[:, :CONV_WIDTH]
        u = z_conv[:, CONV_WIDTH:2 * CONV_WIDTH] * z_conv[:, 2 * CONV_WIDTH:]
        tail = utail_ref[...]
        row = lax.broadcasted_iota(jnp.int32, (rg, CONV_WIDTH), 0)
        u1 = pltpu.roll(u, 1, axis=0)
        u1 = jnp.where(row == 0, tail[7:8, :], u1)
        u2 = pltpu.roll(u, 2, axis=0)
        u2 = jnp.where(row == 0, tail[6:7, :], jnp.where(row == 1, tail[7:8, :], u2))
        utail_ref[...] = u[rg - 8:rg, :]
        cw = cw_ref[...]
        y = cw[0:1, :] * u + cw[1:2, :] * u1 + cw[2:3, :] * u2
        cc = bg * y
        c_ref[0, rows, :] = _rms(cc, cn_ref[...]).astype(bf16)
        yield

        for hh in range(N_HEADS):
            lo = hh * LANES
            qh = qq[:, lo:lo + LANES]
            qh = qh * cos_q + pltpu.roll(qh, HALF_LANES, axis=1) * sin_q
            pair, ps = divmod(hh, HEADS_PER_STEP)
            pl_ = slice(ps * LANES, (ps + 1) * LANES)
            q_ref[0, pair, rows, pl_] = qh.astype(bf16)
            k_ref[0, pair, rows, pl_] = (kk[:, lo:lo + LANES] + kr).astype(bf16)

    live = []
    for r in range(PROJ_ROW_GROUPS):
        live.insert(0, rows_pipeline(slice(r * rg, (r + 1) * rg)))
        for _ in range(2):
            live = [g for g in live if next(g, _DONE) is not _DONE]
    while live:
        live = [g for g in live if next(g, _DONE) is not _DONE]


def _const_spec(shape):
    nd = len(shape)
    return pl.BlockSpec(shape, lambda *_: (0,) * nd, pipeline_mode=pl.Buffered(1))


def _proj_call(x, posf, gpre, wlat, wkr, wconv, qn, kvn, wq, wk, wv, vone, cw, cn, invf):
    B, S, _ = x.shape
    tm = PROJ_TM
    grid = (B, S // tm)
    tok = lambda w: pl.BlockSpec((1, tm, w), lambda b, s: (b, s, 0))
    pos_spec = pl.BlockSpec((1, 1, tm), lambda b, s: (b, 0, s))
    qk_spec = pl.BlockSpec((1, N_PAIRS, tm, PAIR_LANES), lambda b, s: (b, 0, s, 0))
    consts = (gpre, wlat, wkr, wconv, qn, kvn, wq, wk, wv, vone, cw, cn, invf)
    bf16 = jnp.bfloat16
    return pl.pallas_call(
        _proj_kernel,
        grid=grid,
        in_specs=[tok(D_MODEL), pos_spec] + [_const_spec(c.shape) for c in consts],
        out_specs=[qk_spec, qk_spec,
                   pl.BlockSpec((1, VT_WIDTH, tm), lambda b, s: (b, 0, s)),
                   tok(CONV_WIDTH)],
        out_shape=[jax.ShapeDtypeStruct((B, N_PAIRS, S, PAIR_LANES), bf16),
                   jax.ShapeDtypeStruct((B, N_PAIRS, S, PAIR_LANES), bf16),
                   jax.ShapeDtypeStruct((B, VT_WIDTH, S), bf16),
                   jax.ShapeDtypeStruct((B, S, CONV_WIDTH), bf16)],
        scratch_shapes=[pltpu.VMEM((8, CONV_WIDTH), jnp.float32)],
        compiler_params=pltpu.CompilerParams(
            dimension_semantics=("arbitrary", "arbitrary"),
            vmem_limit_bytes=VMEM_LIMIT),
        name="proj",
    )(x, posf, *consts)


def _attn_kernel(q_ref, k_ref, vt_ref, o_ref, s_scr, p_scr):
    S = q_ref.shape[2]
    tq = ATTN_TQ
    bf16 = jnp.bfloat16
    neg = jnp.finfo(jnp.float32).min
    key = lax.broadcasted_iota(jnp.int32, (tq, tq), 0)
    qry = lax.broadcasted_iota(jnp.int32, (tq, tq), 1)
    causal = key <= qry
    nt = S // tq
    bodies = [(j, i, hh) for j in range(PAIRS_PER_TRIP)
              for i in (range(nt) if j % 2 == 0 else range(nt - 1, -1, -1))
              for hh in range(HEADS_PER_STEP)]

    def chunk(c):
        return slice(c * tq, (c + 1) * tq)

    def one_trip(trip, carry):
        col_max = {}
        head_out = {}
        pace = {"tok": jnp.zeros((8, tq), jnp.float32)}

        def scores(n):
            j, i, hh = bodies[n]
            pair = trip * PAIRS_PER_TRIP + j
            hs = slice(hh * LANES, (hh + 1) * LANES)
            q = q_ref[0, pair, chunk(i), hs]
            part = None
            per = -(-(i + 1) // QK_STREAMS)
            starts = list(range(0, i + 1, per))
            streams = [_dot_nt(k_ref[0, pair, c0 * tq:min(c0 + per, i + 1) * tq, hs], q)
                       for c0 in starts]
            order = [c0 + j for j in range(per) for c0 in starts
                     if c0 + j <= min(c0 + per - 1, i)]
            for pos, c in enumerate(order):
                s = streams[c // per][chunk(c % per), :]
                if c == i:
                    s = jnp.where(causal, s, neg)
                s_scr[n % 2, chunk(c), :] = s
                cm = jnp.max(s.reshape(tq // 8, 8, tq), axis=0)
                part = cm if part is None else jnp.maximum(part, cm)
                pace["tok"] = cm
                if pos == len(order) - 1:
                    col_max[n] = jnp.max(part, axis=0, keepdims=True)
                yield

        def exps(n):
            _, i, _ = bodies[n]
            m = col_max.pop(n)
            for c in range(i + 1):
                m_c = m + _dependent_zero(pace["tok"])[0:1, :]
                p_scr[n % 2, chunk(c), :] = jnp.exp2(
                    (s_scr[n % 2, chunk(c), :] - m_c).astype(bf16))
                yield

        def values(n):
            j, i, hh = bodies[n]
            pair = trip * PAIRS_PER_TRIP + j
            acc = None
            for c in range(i + 1):
                vt = vt_ref[0, pair, hh * V_ROWS:(hh + 1) * V_ROWS, chunk(c)]
                d = _dot(vt, p_scr[n % 2, chunk(c), :])
                acc = d if acc is None else acc + d
                if c == i:
                    head_out[hh] = acc[0:V_DIM, :] * (1.0 / acc[V_DIM:V_DIM + 1, :])
                    if hh == HEADS_PER_STEP - 1:
                        o_ref[0, pair, chunk(i), :] = jnp.concatenate(
                            [head_out.pop(h) for h in range(HEADS_PER_STEP)], axis=0).T
                yield

        for step in range(len(bodies) + 2):
            live = []
            if step < len(bodies):
                live.append(scores(step))
            if 0 <= step - 1 < len(bodies):
                live.append(exps(step - 1))
            if 0 <= step - 2 < len(bodies):
                live.append(values(step - 2))
            while live:
                live = [g for g in live if next(g, _DONE) is not _DONE]
        return carry

    lax.fori_loop(0, q_ref.shape[1] // PAIRS_PER_TRIP, one_trip, 0)


def _attn_call(q, k, vt):
    B, _, S, _ = q.shape
    vt = vt.reshape(B, N_PAIRS, HEADS_PER_STEP * V_ROWS, S)
    grid = (B,)
    qk_spec = pl.BlockSpec((1, N_PAIRS, S, PAIR_LANES), lambda b: (b, 0, 0, 0))
    vt_spec = pl.BlockSpec((1, N_PAIRS, HEADS_PER_STEP * V_ROWS, S), lambda b: (b, 0, 0, 0))
    o_spec = pl.BlockSpec((1, N_PAIRS, S, HEADS_PER_STEP * V_DIM), lambda b: (b, 0, 0, 0))
    return pl.pallas_call(
        _attn_kernel,
        grid=grid,
        in_specs=[qk_spec, qk_spec, vt_spec],
        out_specs=o_spec,
        out_shape=jax.ShapeDtypeStruct((B, N_PAIRS, S, HEADS_PER_STEP * V_DIM), jnp.float32),
        scratch_shapes=[pltpu.VMEM((2, S, ATTN_TQ), jnp.float32),
                        pltpu.VMEM((2, S, ATTN_TQ), jnp.bfloat16)],
        compiler_params=pltpu.CompilerParams(
            dimension_semantics=("arbitrary",),
            vmem_limit_bytes=VMEM_LIMIT),
        name="attn",
    )(q, k, vt)


def _post_kernel(x_ref, a_ref, c_ref, p_ref, an_ref, wo_ref, mpn_ref, fpre_ref,
                 wg_ref, wu_ref, wd_ref, fpost_ref, wpp_ref, pn_ref, wpg_ref,
                 o_ref):
    bf16 = jnp.bfloat16
    rg = x_ref.shape[0] // POST_ROW_GROUPS

    def rows_pipeline(rows):
        a = jnp.concatenate([a_ref[0, j, rows, :] for j in range(N_PAIRS)], axis=-1)
        a = _rms(a, an_ref[...]).astype(bf16)
        m = jnp.concatenate([a, c_ref[rows, :]], axis=-1)
        y = _dot(m, wo_ref[...])
        yield
        x1 = x_ref[rows, :] + _rms(y, mpn_ref[...])
        h = _rms(x1, fpre_ref[...]).astype(bf16)
        yield
        f = None
        for j in range(D_FF // FF_CHUNK):
            cs = slice(j * FF_CHUNK, (j + 1) * FF_CHUNK)
            g = _dot(h, wg_ref[:, cs])
            u = _dot(h, wu_ref[:, cs])
            act = (g * jax.nn.sigmoid(g) * u).astype(bf16)
            part = _dot(act, wd_ref[cs, :])
            f = part if f is None else f + part
            yield
        x2 = x1 + _rms(f, fpost_ref[...])
        e = _rms(_dot(p_ref[rows, :].astype(bf16), wpp_ref[...]), pn_ref[...])
        gate = jax.nn.sigmoid(_dot(x2.astype(bf16), wpg_ref[...]))
        o_ref[rows, :] = x2 + gate * e

    live = []
    for r in range(POST_ROW_GROUPS):
        live.insert(0, rows_pipeline(slice(r * rg, (r + 1) * rg)))
        live = [g for g in live if next(g, _DONE) is not _DONE]
    while live:
        live = [g for g in live if next(g, _DONE) is not _DONE]


def _post_call(x2d, a4d, c2d, p2d, an, wo, mpn, fpre, wg, wu, wd, fpost, wpp, pn, wpg):
    T = x2d.shape[0]
    tm = POST_TM
    tiles_per_row = a4d.shape[2] // tm
    tok = lambda w: pl.BlockSpec((tm, w), lambda t: (t, 0))
    a_spec = pl.BlockSpec((1, N_PAIRS, tm, a4d.shape[3]),
                          lambda t: (t // tiles_per_row, 0, t % tiles_per_row, 0))
    consts = (an, wo, mpn, fpre, wg, wu, wd, fpost, wpp, pn, wpg)
    return pl.pallas_call(
        _post_kernel,
        grid=(T // tm,),
        in_specs=[tok(D_MODEL), a_spec, tok(CONV_WIDTH), tok(PLE_DIM)]
        + [_const_spec(c.shape) for c in consts],
        out_specs=tok(D_MODEL),
        out_shape=jax.ShapeDtypeStruct((T, D_MODEL), jnp.float32),
        compiler_params=pltpu.CompilerParams(
            dimension_semantics=("arbitrary",),
            vmem_limit_bytes=POST_VMEM_LIMIT),
        name="post",
    )(x2d, a4d, c2d, p2d, *consts)


def _inv_freq_column():
    inv_freq = 1.0 / (ROPE_THETA ** (np.arange(HALF_ROPE, dtype=np.float32) / HALF_ROPE))
    return jnp.asarray(inv_freq.astype(np.float32).reshape(HALF_ROPE, 1))


def _head_lanes(nope, rope):
    ref = nope if nope is not None else rope
    lead = ref.shape[:-1]
    zeros = lambda n: jnp.zeros(lead + (n,), ref.dtype)
    n_lo = HALF_LANES - HALF_ROPE
    parts = [
        rope[..., :HALF_ROPE] if rope is not None else zeros(HALF_ROPE),
        nope[..., :n_lo] if nope is not None else zeros(n_lo),
        rope[..., HALF_ROPE:] if rope is not None else zeros(HALF_ROPE),
        nope[..., n_lo:] if nope is not None else zeros(NOPE_DIM - n_lo),
        zeros(LANES - NOPE_DIM - ROPE_DIM),
    ]
    return jnp.concatenate(parts, axis=-1)


def _prep_w_in(w_in):
    o2 = Q_RANK + KV_RANK
    o3 = o2 + ROPE_DIM
    bf16 = jnp.bfloat16
    kr = _head_lanes(None, w_in[:, o2:o3])
    return w_in[:, :o2].astype(bf16), kr.astype(bf16), w_in[:, o3:].astype(bf16)


def _prep_w_uq(w_uq):
    w = w_uq.reshape(Q_RANK, N_HEADS, NOPE_DIM + ROPE_DIM)
    wq = _head_lanes(w[..., :NOPE_DIM], w[..., NOPE_DIM:])
    return wq.reshape(Q_RANK, QK_WIDTH).astype(jnp.bfloat16)


def _prep_w_ukv(w_ukv):
    w = w_ukv.reshape(KV_RANK, N_HEADS, NOPE_DIM + V_DIM)
    k_nope, v = w[..., :NOPE_DIM], w[..., NOPE_DIM:]
    wk = _head_lanes(k_nope, None).reshape(KV_RANK, QK_WIDTH)
    v_pad = jnp.concatenate([v, jnp.zeros((KV_RANK, N_HEADS, SUM_ROWS), w.dtype)], axis=-1)
    wv_t = v_pad.reshape(KV_RANK, VT_WIDTH).T
    return wk.astype(jnp.bfloat16), wv_t.astype(jnp.bfloat16)


def _ones_rows_column():
    col = np.zeros((N_HEADS, V_ROWS, 1), np.float32)
    col[:, V_DIM:, :] = 1.0
    return jnp.asarray(col.reshape(VT_WIDTH, 1))


def kernel(x, p, positions, mix_pre_norm, w_in, q_norm, kv_norm, w_uq, w_ukv, conv_w, attn_group_norm, conv_group_norm, w_out, mix_post_norm, ffn_pre_norm, w_gate, w_up, w_down, ffn_post_norm, w_ple_proj, ple_norm, w_ple_gate):
    B, S, D = x.shape
    depth = w_in.shape[0]
    bf16 = jnp.bfloat16
    row = lambda g: g.reshape(1, -1)
    invf = _inv_freq_column()
    vone = _ones_rows_column()
    posf = positions.astype(jnp.float32)[:, None, :]
    for i in range(depth):
        wk, wv = _prep_w_ukv(w_ukv[i])
        q, k, v, c = _proj_call(
            x, posf, row(mix_pre_norm[i]), *_prep_w_in(w_in[i]), row(q_norm[i]),
            row(kv_norm[i]), _prep_w_uq(w_uq[i]), wk, wv, vone, conv_w[i],
            row(conv_group_norm[i]), invf)
        a = _attn_call(q, k, v)
        out = _post_call(
            x.reshape(B * S, D), a,
            c.reshape(B * S, CONV_WIDTH), p[i].reshape(B * S, PLE_DIM),
            row(attn_group_norm[i]), w_out[i].astype(bf16), row(mix_post_norm[i]),
            row(ffn_pre_norm[i]), w_gate[i].astype(bf16), w_up[i].astype(bf16),
            w_down[i].astype(bf16), row(ffn_post_norm[i]),
            w_ple_proj[i].astype(bf16), row(ple_norm[i]), w_ple_gate[i].astype(bf16))
        x = out.reshape(B, S, D)
    return x
```
